```python
import jax
import jax.numpy as jnp
from jax import lax
import numpy as np

D_MODEL = 2048
BATCH = 16
SEQ = 256
DEPTH = 4
DEC_BATCH = 8
DEC_SEQ = 4096
PAST_LEN = 256

GRID_W = 64
D_MIX = D_MODEL
N_HEADS = D_MODEL // 256
Q_LORA = D_MODEL // 4
KV_LORA = D_MODEL // 8
QK_NOPE = 128
QK_ROPE = 64
QK_DIM = QK_NOPE + QK_ROPE
V_DIM = 128
D_ATT = N_HEADS * V_DIM
D_SC = D_MIX // 4
SC_WIDTH = 3
D_CF = D_MIX - D_ATT - D_SC
CF_WIDTH = 31
N_EXPERTS = 16
EC_CAPACITY = 2
D_FF_EXPERT = D_MODEL // 2
ROPE_BASE = 10000.0
NORM_EPS = 1e-6
Q_BLOCK = 128
ATTN_SCALE = QK_DIM ** -0.5

_S0 = Q_LORA
_S1 = _S0 + KV_LORA
_S2 = _S1 + QK_ROPE
_S3 = _S2 + D_SC
_S4 = _S3 + D_SC
_S5 = _S4 + D_SC
_S6 = _S5 + D_CF
IN_SPLITS = (_S0, _S1, _S2, _S3, _S4, _S5, _S6)
P_IN = _S6 + D_CF

kernel_name = 'hybrid_mla_conv_ec_diffusion_step'


def _rmsnorm(x, g):
    xf = x.astype(jnp.float32)
    y = xf * lax.rsqrt(jnp.mean(xf * xf, axis=-1, keepdims=True) + NORM_EPS)
    return (y * g.astype(jnp.float32)).astype(x.dtype)


def _layernorm(x, g, b):
    xf = x.astype(jnp.float32)
    mu = jnp.mean(xf, axis=-1, keepdims=True)
    var = jnp.mean(jnp.square(xf - mu), axis=-1, keepdims=True)
    y = (xf - mu) * lax.rsqrt(var + NORM_EPS)
    return (y * g.astype(jnp.float32) + b.astype(jnp.float32)).astype(x.dtype)


def _grid_rope_tables(n_tokens, dtype):
    rows = n_tokens // GRID_W
    row = jnp.repeat(jnp.arange(rows, dtype=jnp.float32), GRID_W)
    col = jnp.tile(jnp.arange(GRID_W, dtype=jnp.float32), rows)
    n_pairs = QK_ROPE // 4
    inv = ROPE_BASE ** (-jnp.arange(n_pairs, dtype=jnp.float32) / n_pairs)
    ang_r = row[:, None] * inv
    ang_c = col[:, None] * inv
    return (jnp.cos(ang_r).astype(dtype), jnp.sin(ang_r).astype(dtype),
            jnp.cos(ang_c).astype(dtype), jnp.sin(ang_c).astype(dtype))


def _rotate(x, cos, sin):
    x1, x2 = jnp.split(x, 2, axis=-1)
    return jnp.concatenate([x1 * cos - x2 * sin, x2 * cos + x1 * sin], axis=-1)


def _rope_2d(x, tables):
    cos_r, sin_r, cos_c, sin_c = tables
    shape = (x.shape[1],) + (1,) * (x.ndim - 3) + (cos_r.shape[-1],)
    xr, xc = jnp.split(x, 2, axis=-1)
    return jnp.concatenate([
        _rotate(xr, cos_r.reshape(shape), sin_r.reshape(shape)),
        _rotate(xc, cos_c.reshape(shape), sin_c.reshape(shape))], axis=-1)


def _depthwise_conv(z, w):
    k, ch = w.shape
    pad = (k - 1) // 2
    return lax.conv_general_dilated(
        z, w[:, None, :].astype(z.dtype), window_strides=(1,), padding=[(pad, pad)],
        dimension_numbers=('NWC', 'WIO', 'NWC'), feature_group_count=ch)


def _attend(q, k, v):
    b, lq, h, dk = q.shape
    nb = lq // Q_BLOCK
    qb = q.reshape(b, nb, Q_BLOCK, h, dk).transpose(1, 0, 2, 3, 4)

    def one_block(qi):
        s = jnp.einsum('bqhd,bkhd->bhqk', qi, k, preferred_element_type=jnp.float32) * ATTN_SCALE
        p = jax.nn.softmax(s, axis=-1).astype(v.dtype)
        return jnp.einsum('bhqk,bkhd->bqhd', p, v)

    o = lax.map(one_block, qb)
    return o.transpose(1, 0, 2, 3, 4).reshape(b, lq, h, v.shape[-1])


def _token_mixing(h, w_in, q_norm_g, w_uq, kv_norm_g, w_ukv, sc_conv_w, cf_conv_w, cf_conv_b,
                  cf_ln_g, cf_ln_b, w_o, rope_tables=None, ctx_ckv=None, ctx_kr=None):
    b, l = h.shape[:2]
    u = jnp.einsum('bld,dp->blp', h, w_in)
    cq, ckv, kr, sc_x, sc_b, sc_c, cf_a, cf_g = jnp.split(u, IN_SPLITS, axis=-1)
    q = jnp.einsum('blr,rhd->blhd', _rmsnorm(cq, q_norm_g), w_uq)
    q_nope, q_rope = q[..., :QK_NOPE], q[..., QK_NOPE:]
    ckv = _rmsnorm(ckv, kv_norm_g)
    if rope_tables is None:
        keys_c, keys_r = ckv, kr
    else:
        q_rope = _rope_2d(q_rope, rope_tables)
        keys_c = jnp.concatenate([ckv, ctx_ckv.astype(ckv.dtype)], axis=1)
        keys_r = jnp.concatenate([_rope_2d(kr, rope_tables), ctx_kr.astype(kr.dtype)], axis=1)
    kv = jnp.einsum('blr,rhd->blhd', keys_c, w_ukv)
    k_nope, v = kv[..., :QK_NOPE], kv[..., QK_NOPE:]
    k = jnp.concatenate(
        [k_nope, jnp.broadcast_to(keys_r[:, :, None, :], k_nope.shape[:3] + (QK_ROPE,))], axis=-1)
    o_att = _attend(jnp.concatenate([q_nope, q_rope], axis=-1), k, v).reshape(b, l, D_ATT)
    y_sc = sc_b * _depthwise_conv(sc_c * sc_x, sc_conv_w)
    z = cf_a * jax.nn.sigmoid(cf_g)
    z = _depthwise_conv(z, cf_conv_w) + cf_conv_b
    z = jax.nn.silu(_layernorm(z, cf_ln_g, cf_ln_b))
    y = jnp.einsum('blm,md->bld', jnp.concatenate([o_att, y_sc, z], axis=-1), w_o)
    return y, ckv, kr


def _ec_moe(h, router_w, w_gate, w_up, w_down):
    b, l, d = h.shape
    cap = max(1, EC_CAPACITY * l // N_EXPERTS)
    logits = jnp.einsum('bld,de->ble', h, router_w, preferred_element_type=jnp.float32)
    aff = jax.nn.softmax(logits, axis=-1)
    g, idx = lax.top_k(aff.transpose(0, 2, 1), cap)
    xs = jax.vmap(lambda hb, ib: hb[ib])(h, idx)
    hid = jax.nn.silu(jnp.einsum('becd,edf->becf', xs, w_gate)) * jnp.einsum('becd,edf->becf', xs, w_up)
    out = jnp.einsum('becf,efd->becd', hid, w_down) * g.astype(h.dtype)[..., None]
    return jax.vmap(lambda ib, ob: jnp.zeros((l, d), ob.dtype).at[ib.reshape(-1)].add(ob.reshape(-1, d)))(idx, out)


def _layer(x, mod, n1, n2, mix_w, moe_w, rope_tables=None, ctx_ckv=None, ctx_kr=None):
    sh1, sc1, g1, sh2, sc2, g2 = jnp.split(mod, 6, axis=-1)
    h = _rmsnorm(x, n1) * (1 + sc1) + sh1
    y, ckv, kr = _token_mixing(h, *mix_w, rope_tables=rope_tables, ctx_ckv=ctx_ckv, ctx_kr=ctx_kr)
    x = x + g1 * y
    h = _rmsnorm(x, n2) * (1 + sc2) + sh2
    x = x + g2 * _ec_moe(h, *moe_w)
    return x, ckv, kr


def setup_inputs(seed: int = 0) -> dict:
    key = jax.random.key(seed)
    ks = jax.random.split(key, 32)

    def nrm(k, shape, scale):
        return jax.random.normal(k, shape, jnp.float32) * scale

    return {
        'x_prompt': nrm(ks[0], (BATCH, SEQ, D_MODEL), 1.0),
        'x_sample': nrm(ks[1], (DEC_BATCH, DEC_SEQ, D_MODEL), 1.0),
        'cache_ckv': nrm(ks[2], (DEC_BATCH, DEPTH, PAST_LEN, KV_LORA), 1.0),
        'cache_krope': nrm(ks[3], (DEC_BATCH, DEPTH, PAST_LEN, QK_ROPE), 1.0),
        'c': nrm(ks[4], (DEC_BATCH, D_MODEL), 1.0),
        'c_ctx': nrm(ks[5], (D_MODEL,), 1.0),
        'ada_w': nrm(ks[6], (DEPTH, D_MODEL, 6 * D_MODEL), 0.5 * D_MODEL ** -0.5),
        'ada_b': nrm(ks[7], (DEPTH, 6 * D_MODEL), 0.01),
        'norm1_g': 1.0 + nrm(ks[8], (DEPTH, D_MODEL), 0.05),
        'norm2_g': 1.0 + nrm(ks[9], (DEPTH, D_MODEL), 0.05),
        'w_in': nrm(ks[10], (DEPTH, D_MODEL, P_IN), D_MODEL ** -0.5),
        'q_norm_g': 1.0 + nrm(ks[11], (DEPTH, Q_LORA), 0.05),
        'w_uq': nrm(ks[12], (DEPTH, Q_LORA, N_HEADS, QK_DIM), Q_LORA ** -0.5),
        'kv_norm_g': 1.0 + nrm(ks[13], (DEPTH, KV_LORA), 0.05),
        'w_ukv': nrm(ks[14], (DEPTH, KV_LORA, N_HEADS, QK_NOPE + V_DIM), KV_LORA ** -0.5),
        'sc_conv_w': nrm(ks[15], (DEPTH, SC_WIDTH, D_SC), SC_WIDTH ** -0.5),
        'cf_conv_w': nrm(ks[16], (DEPTH, CF_WIDTH, D_CF), CF_WIDTH ** -0.5),
        'cf_conv_b': nrm(ks[17], (DEPTH, D_CF), 0.01),
        'cf_ln_g': 1.0 + nrm(ks[18], (DEPTH, D_CF), 0.05),
        'cf_ln_b': nrm(ks[19], (DEPTH, D_CF), 0.01),
        'w_o': nrm(ks[20], (DEPTH, D_MIX, D_MODEL), D_MIX ** -0.5),
        'router_w': nrm(ks[21], (DEPTH, D_MODEL, N_EXPERTS), D_MODEL ** -0.5),
        'exp_w_gate': nrm(ks[22], (DEPTH, N_EXPERTS, D_MODEL, D_FF_EXPERT), D_MODEL ** -0.5),
        'exp_w_up': nrm(ks[23], (DEPTH, N_EXPERTS, D_MODEL, D_FF_EXPERT), D_MODEL ** -0.5),
        'exp_w_down': nrm(ks[24], (DEPTH, N_EXPERTS, D_FF_EXPERT, D_MODEL), D_FF_EXPERT ** -0.5),
        'final_norm_g': 1.0 + nrm(ks[25], (D_MODEL,), 0.05),
    }


def reference(x_prompt, x_sample, cache_ckv, cache_krope, c, c_ctx, ada_w, ada_b, norm1_g, norm2_g,
              w_in, q_norm_g, w_uq, kv_norm_g, w_ukv, sc_conv_w, cf_conv_w, cf_conv_b, cf_ln_g,
              cf_ln_b, w_o, router_w, exp_w_gate, exp_w_up, exp_w_down, final_norm_g):
    rope_tables = _grid_rope_tables(x_sample.shape[1], x_sample.dtype)
    s_ctx = jax.nn.silu(c_ctx)
    s_lat = jax.nn.silu(c)
    xp, xs = x_prompt, x_sample
    new_ckv, new_kr = [], []
    for l in range(DEPTH):
        mix_w = (w_in[l], q_norm_g[l], w_uq[l], kv_norm_g[l], w_ukv[l], sc_conv_w[l], cf_conv_w[l],
                 cf_conv_b[l], cf_ln_g[l], cf_ln_b[l], w_o[l])
        moe_w = (router_w[l], exp_w_gate[l], exp_w_up[l], exp_w_down[l])
        mod_p = (s_ctx @ ada_w[l] + ada_b[l])[None, None, :]
        mod_s = (s_lat @ ada_w[l] + ada_b[l])[:, None, :]
        xp, ckv_l, kr_l = _layer(xp, mod_p, norm1_g[l], norm2_g[l], mix_w, moe_w)
        new_ckv.append(ckv_l)
        new_kr.append(kr_l)
        xs, _, _ = _layer(xs, mod_s, norm1_g[l], norm2_g[l], mix_w, moe_w, rope_tables=rope_tables,
                          ctx_ckv=cache_ckv[:, l], ctx_kr=cache_krope[:, l])
    y_prompt = _rmsnorm(xp, final_norm_g)
    y_sample = _rmsnorm(xs, final_norm_g)
    new_ckv_arr = jnp.stack(new_ckv, axis=1)
    new_kr_arr = jnp.stack(new_kr, axis=1)
    return (y_prompt, y_sample, new_ckv_arr, new_kr_arr)
```

```python
import functools

import jax
import jax.numpy as jnp
from jax import lax
from jax.experimental import pallas as pl
from jax.experimental.pallas import tpu as pltpu

LANES = 128
SUBLANES = 8
BF16_SUBLANES = 16
VMEM_LIMIT_BYTES = 56 * 1024 * 1024

N_HEADS = 8
QK_NOPE = 128
QK_ROPE = 64
V_DIM = 128
HEAD_SLAB = 256
N_EXPERTS = 16
EC_CAPACITY = 2
GRID_W = 64
SC_WIDTH = 3
CF_WIDTH = 31
CF_PAD = (CF_WIDTH - 1) // 2
HALO = 16
ROPE_BASE = 10000.0
NORM_EPS = 1e-6
ATTN_SCALE = (QK_NOPE + QK_ROPE) ** -0.5

F32 = jnp.float32
BF16 = jnp.bfloat16


def _params(sem):
    return pltpu.CompilerParams(dimension_semantics=sem, vmem_limit_bytes=VMEM_LIMIT_BYTES)


def _sigmoid(x):
    return 1.0 / (1.0 + jnp.exp(-x))


def _rms(x, g):
    return x * lax.rsqrt(jnp.mean(x * x, axis=-1, keepdims=True) + NORM_EPS) * g


def _split3_dot(a, b, dims):
    a_hi = a.astype(BF16)
    a_lo = (a - a_hi.astype(F32)).astype(BF16)
    b_hi = b.astype(BF16)
    b_lo = (b - b_hi.astype(F32)).astype(BF16)
    dot = functools.partial(lax.dot_general, dimension_numbers=(dims, ((), ())), preferred_element_type=F32)
    return dot(a_hi, b_hi) + dot(a_lo, b_hi) + dot(a_hi, b_lo)


def _ada_kernel(c_ref, w_ref, b_ref, o_ref):
    cv = c_ref[...]
    s = cv * _sigmoid(cv)
    o_ref[0] = _split3_dot(s, w_ref[0], ((1,), (0,))) + b_ref[0]


def _ada_mod(c_all, ada_w, ada_b):
    depth, d, n6 = ada_w.shape
    rows = c_all.shape[0]
    tn = 1024
    return pl.pallas_call(
        _ada_kernel,
        grid=(depth, n6 // tn),
        in_specs=[pl.BlockSpec((rows, d), lambda l, j: (0, 0)),
                  pl.BlockSpec((1, d, tn), lambda l, j: (l, 0, j)),
                  pl.BlockSpec((1, 1, tn), lambda l, j: (l, 0, j))],
        out_specs=pl.BlockSpec((1, rows, tn), lambda l, j: (l, 0, j)),
        out_shape=jax.ShapeDtypeStruct((depth, rows, n6), F32),
        compiler_params=_params(("parallel", "parallel")),
        name="ada_mod",
    )(c_all, ada_w, ada_b.reshape(depth, 1, n6))


def _moe_residual(x, moe_ref, g2, tm):
    chunks = []
    for p in range(2):
        for c in range(SUBLANES):
            chunks.append(moe_ref[0, p, pl.ds(c, tm, stride=SUBLANES), :])
    return x + g2 * jnp.concatenate(chunks, axis=1)


def _rope_pair(v, tab):
    prod = v * tab
    r = prod + pltpu.roll(prod, QK_ROPE, 1)
    lane = lax.broadcasted_iota(jnp.int32, r.shape, 1)
    return jnp.where(lane < QK_ROPE, r, 0.0)


def _inproj_kernel(has_moe, tm, d_model, q_lora, kv_lora, *refs):
    if has_moe:
        (x_ref, moe_ref, g2_ref, shsc_ref, n1_ref, tab_ref, win_ref, qg_ref, wuq_ref, kvg_ref, wukv_ref,
         xo_ref, q_ref, k_ref, v_ref, cv_ref, ckv_ref, kr_ref) = refs
        x = _moe_residual(x_ref[...], moe_ref, g2_ref[0], tm)
        xo_ref[...] = x
    else:
        (x_ref, shsc_ref, n1_ref, tab_ref, win_ref, qg_ref, wuq_ref, kvg_ref, wukv_ref,
         q_ref, k_ref, v_ref, cv_ref, ckv_ref, kr_ref) = refs
        x = x_ref[...]
    sh1 = shsc_ref[0, :, :d_model]
    sc1 = shsc_ref[0, :, d_model:]
    h = _rms(x, n1_ref[...]) * (1.0 + sc1) + sh1
    u = jnp.dot(h.astype(BF16), win_ref[...], preferred_element_type=F32)
    tab = tab_ref[...]

    cqn = _rms(u[:, :q_lora], qg_ref[...])
    qf = jnp.dot(cqn.astype(BF16), wuq_ref[...], preferred_element_type=F32)
    for hd in range(N_HEADS):
        o = hd * HEAD_SLAB
        q_ref[:, o:o + QK_NOPE] = (qf[:, o:o + QK_NOPE] * ATTN_SCALE).astype(BF16)
        q_ref[:, o + QK_NOPE:o + HEAD_SLAB] = (_rope_pair(qf[:, o + QK_NOPE:o + HEAD_SLAB], tab) * ATTN_SCALE).astype(BF16)

    c0 = q_lora
    ckvn = _rms(u[:, c0:c0 + kv_lora], kvg_ref[...])
    ckv_ref[...] = ckvn
    kvf = jnp.dot(ckvn.astype(BF16), wukv_ref[...], preferred_element_type=F32)
    c1 = c0 + kv_lora
    krv = u[:, c1:c1 + LANES]
    kr_ref[...] = krv[:, :QK_ROPE]
    kr2 = _rope_pair(krv, tab).astype(BF16)
    for hd in range(N_HEADS):
        o = hd * HEAD_SLAB
        k_ref[:, o:o + QK_NOPE] = kvf[:, hd * QK_NOPE:(hd + 1) * QK_NOPE].astype(BF16)
        k_ref[:, o + QK_NOPE:o + HEAD_SLAB] = kr2
    v_ref[...] = kvf[:, N_HEADS * QK_NOPE:].astype(BF16)
    cv_ref[...] = u[:, c1 + LANES:]


def _inproj(x, moe, mod_prev, mod, n1, tab, lw, *, tm, tiles_per_mod, tiles_per_seq):
    t, d = x.shape
    nt = t // tm
    p_in = lw["w_in"].shape[1]
    q_lora = lw["q_norm_g"].shape[1]
    kv_lora = lw["kv_norm_g"].shape[1]
    n_cv = p_in - q_lora - kv_lora - LANES
    has_moe = moe is not None
    tiles_per_row = None if moe is None else moe.shape[2] // (tm * SUBLANES)

    full = lambda a: pl.BlockSpec(a.shape, lambda i: (0,) * a.ndim)
    row = lambda w: pl.BlockSpec((tm, w), lambda i: (i, 0))
    in_specs = [row(d)]
    args = [x]
    if has_moe:
        in_specs += [pl.BlockSpec((1, 2, tm * SUBLANES, LANES), lambda i: (i // tiles_per_row, 0, i % tiles_per_row, 0)),
                     pl.BlockSpec((1, 1, d), lambda i: (i // tiles_per_mod, 0, 5))]
        args += [moe, mod_prev]
    in_specs += [pl.BlockSpec((1, 1, 2 * d), lambda i: (i // tiles_per_mod, 0, 0)),
                 full(n1),
                 pl.BlockSpec((tm, LANES), lambda i: (i % tiles_per_seq, 0)),
                 full(lw["w_in"]), full(lw["q_norm_g"]), full(lw["w_uq"]), full(lw["kv_norm_g"]), full(lw["w_ukv"])]
    args += [mod, n1, tab, lw["w_in"], lw["q_norm_g"], lw["w_uq"], lw["kv_norm_g"], lw["w_ukv"]]

    out_specs, out_shape = [], []
    if has_moe:
        out_specs.append(row(d))
        out_shape.append(jax.ShapeDtypeStruct((t, d), F32))
    for w, dt in ((N_HEADS * HEAD_SLAB, BF16), (N_HEADS * HEAD_SLAB, BF16), (N_HEADS * V_DIM, BF16),
                  (n_cv, F32), (kv_lora, F32), (QK_ROPE, F32)):
        out_specs.append(row(w))
        out_shape.append(jax.ShapeDtypeStruct((t, w), dt))

    outs = pl.pallas_call(
        functools.partial(_inproj_kernel, has_moe, tm, d, q_lora, kv_lora),
        grid=(nt,), in_specs=in_specs, out_specs=out_specs, out_shape=out_shape,
        compiler_params=_params(("parallel",)), name="inproj",
    )(*args)
    if not has_moe:
        outs = [x] + list(outs)
    return outs


def _cachekv_kernel(ckv_ref, kr_ref, wukv_ref, k_ref, v_ref):
    kvf = jnp.dot(ckv_ref[0].astype(BF16), wukv_ref[0], preferred_element_type=F32)
    kr2 = kr_ref[0].astype(BF16)
    for hd in range(N_HEADS):
        o = hd * HEAD_SLAB
        k_ref[0, :, o:o + QK_NOPE] = kvf[:, hd * QK_NOPE:(hd + 1) * QK_NOPE].astype(BF16)
        k_ref[0, :, o + QK_NOPE:o + HEAD_SLAB] = kr2
    v_ref[0] = kvf[:, N_HEADS * QK_NOPE:].astype(BF16)


def _cache_kv(ckv_c, kr_c, w_ukv_all, tm):
    depth, rows, kv_lora = ckv_c.shape
    nk = N_HEADS * HEAD_SLAB
    nv = N_HEADS * V_DIM
    return pl.pallas_call(
        _cachekv_kernel,
        grid=(depth, rows // tm),
        in_specs=[pl.BlockSpec((1, tm, kv_lora), lambda l, i: (l, i, 0)),
                  pl.BlockSpec((1, tm, LANES), lambda l, i: (l, i, 0)),
                  pl.BlockSpec((1, kv_lora, nk // 2 + nv), lambda l, i: (l, 0, 0))],
        out_specs=[pl.BlockSpec((1, tm, nk), lambda l, i: (l, i, 0)),
                   pl.BlockSpec((1, tm, nv), lambda l, i: (l, i, 0))],
        out_shape=[jax.ShapeDtypeStruct((depth, rows, nk), BF16), jax.ShapeDtypeStruct((depth, rows, nv), BF16)],
        compiler_params=_params(("parallel", "parallel")), name="cache_kv",
    )(ckv_c, kr_c, w_ukv_all)


def _attn_kernel(has_ctx, lk, tk, *refs):
    if has_ctx:
        q_ref, k_ref, v_ref, kc_ref, vc_ref, o_ref = refs
    else:
        q_ref, k_ref, v_ref, o_ref = refs
    q = q_ref[...]
    tq = q.shape[0]

    def step(carry, kc, vc):
        m, l, acc = carry
        s = lax.dot_general(q, kc, (((1,), (1,)), ((), ())), preferred_element_type=F32)
        m_new = jnp.maximum(m, jnp.max(s, axis=-1, keepdims=True))
        alpha = jnp.exp(m - m_new)
        p = jnp.exp(s - m_new)
        l = alpha * l + jnp.sum(p, axis=-1, keepdims=True)
        acc = alpha * acc + jnp.dot(p.astype(BF16), vc, preferred_element_type=F32)
        return m_new, l, acc

    carry = (jnp.full((tq, 1), -jnp.inf, F32), jnp.zeros((tq, 1), F32), jnp.zeros((tq, V_DIM), F32))

    def body(j, carry):
        r0 = pl.multiple_of(j * tk, tk)
        return step(carry, k_ref[pl.ds(r0, tk), :], v_ref[pl.ds(r0, tk), :])

    carry = lax.fori_loop(0, lk // tk, body, carry)
    if has_ctx:
        carry = step(carry, kc_ref[0], vc_ref[0])
    _, l, acc = carry
    o_ref[...] = (acc / l).astype(BF16)


def _attention(q, k, v, kc, vc, *, batch, seq, tq, tk):
    t = q.shape[0]
    nq = seq // tq
    has_ctx = kc is not None
    in_specs = [pl.BlockSpec((tq, HEAD_SLAB), lambda b, h, i: (b * nq + i, h)),
                pl.BlockSpec((seq, HEAD_SLAB), lambda b, h, i: (b, h)),
                pl.BlockSpec((seq, V_DIM), lambda b, h, i: (b, h))]
    args = [q, k, v]
    if has_ctx:
        past = kc.shape[1]
        in_specs += [pl.BlockSpec((1, past, HEAD_SLAB), lambda b, h, i: (b, 0, h)),
                     pl.BlockSpec((1, past, V_DIM), lambda b, h, i: (b, 0, h))]
        args += [kc, vc]
    return pl.pallas_call(
        functools.partial(_attn_kernel, has_ctx, seq, tk),
        grid=(batch, N_HEADS, nq), in_specs=in_specs,
        out_specs=pl.BlockSpec((tq, V_DIM), lambda b, h, i: (b * nq + i, h)),
        out_shape=jax.ShapeDtypeStruct((t, N_HEADS * V_DIM), BF16),
        compiler_params=_params(("parallel", "parallel", "arbitrary")), name="attention",
    )(*args)


def _mixout_kernel(tm, tiles_per_seq, d_sc, d_cf,
                   o_ref, cv_ref, cvp_ref, cvn_ref, x_ref, g1_ref, sh2_ref, sc2_ref,
                   scw_ref, cfw_ref, cfb_ref, lng_ref, lnb_ref, wo_ref, n2_ref, rwt_ref,
                   xo_ref, hg_ref, lg_ref, ext_sc, ext_cf, hg_scr):
    i = pl.program_id(0)
    first = (i % tiles_per_seq) == 0
    last = (i % tiles_per_seq) == tiles_per_seq - 1
    o_scx, o_scb, o_scc, o_cfa, o_cfg = 0, d_sc, 2 * d_sc, 3 * d_sc, 3 * d_sc + d_cf

    def sc_in(ref):
        return ref[:, o_scc:o_scc + d_sc] * ref[:, o_scx:o_scx + d_sc]

    def cf_in(ref):
        return ref[:, o_cfa:o_cfa + d_cf] * _sigmoid(ref[:, o_cfg:o_cfg + d_cf])

    ext_sc[0:HALO, :] = jnp.where(first, 0.0, sc_in(cvp_ref))
    ext_sc[HALO:HALO + tm, :] = sc_in(cv_ref)
    ext_sc[HALO + tm:, :] = jnp.where(last, 0.0, sc_in(cvn_ref))
    ext_cf[0:HALO, :] = jnp.where(first, 0.0, cf_in(cvp_ref))
    ext_cf[HALO:HALO + tm, :] = cf_in(cv_ref)
    ext_cf[HALO + tm:, :] = jnp.where(last, 0.0, cf_in(cvn_ref))

    sc_pad = (SC_WIDTH - 1) // 2
    y_sc = jnp.zeros((tm, d_sc), F32)
    for kk in range(SC_WIDTH):
        y_sc = y_sc + scw_ref[kk:kk + 1, :] * ext_sc[pl.ds(HALO - sc_pad + kk, tm), :]
    y_sc = cv_ref[:, o_scb:o_scb + d_sc] * y_sc

    z = jnp.zeros((tm, d_cf), F32)
    for kk in range(CF_WIDTH):
        z = z + cfw_ref[kk:kk + 1, :] * ext_cf[pl.ds(HALO - CF_PAD + kk, tm), :]
    z = z + cfb_ref[...]
    mu = jnp.mean(z, axis=-1, keepdims=True)
    zc = z - mu
    var = jnp.mean(zc * zc, axis=-1, keepdims=True)
    z = zc * lax.rsqrt(var + NORM_EPS) * lng_ref[...] + lnb_ref[...]
    z = z * _sigmoid(z)

    mixed = jnp.concatenate([o_ref[...], y_sc.astype(BF16), z.astype(BF16)], axis=1)
    y = jnp.dot(mixed, wo_ref[...], preferred_element_type=F32)
    x = x_ref[...] + g1_ref[0] * y
    xo_ref[...] = x
    h2 = _rms(x, n2_ref[...]) * (1.0 + sc2_ref[0]) + sh2_ref[0]

    lg_ref[0] = _split3_dot(rwt_ref[...], h2, ((1,), (1,)))

    d_model = h2.shape[1]
    for c in range(d_model // LANES):
        hg_scr[pl.ds(c, tm, stride=BF16_SUBLANES), :] = h2[:, c * LANES:(c + 1) * LANES]
    hg_ref[...] = hg_scr[...].astype(BF16)


def _mixout(o_att, cv, x, mod, lw, *, tm, tiles_per_mod, tiles_per_seq, route_batch, route_len):
    t, d = x.shape
    nt = t // tm
    n_cv = cv.shape[1]
    d_sc = lw["sc_conv_w"].shape[1]
    d_cf = lw["cf_conv_w"].shape[1]
    hb = tm // HALO
    n_halo = t // HALO
    tiles_per_route = route_len // tm
    full = lambda a: pl.BlockSpec(a.shape, lambda i: (0,) * a.ndim)
    row = lambda w: pl.BlockSpec((tm, w), lambda i: (i, 0))
    modspec = lambda blk: pl.BlockSpec((1, 1, d), lambda i: (i // tiles_per_mod, 0, blk))
    in_specs = [row(o_att.shape[1]), row(n_cv),
                pl.BlockSpec((HALO, n_cv), lambda i: (jnp.maximum(i * hb - 1, 0), 0)),
                pl.BlockSpec((HALO, n_cv), lambda i: (jnp.minimum((i + 1) * hb, n_halo - 1), 0)),
                row(d), modspec(2), modspec(3), modspec(4),
                full(lw["sc_conv_w"]), full(lw["cf_conv_w"]), full(lw["cf_conv_b"]), full(lw["cf_ln_g"]),
                full(lw["cf_ln_b"]), full(lw["w_o"]), full(lw["norm2_g"]), full(lw["router_wt"])]
    args = [o_att, cv, cv, cv, x, mod, mod, mod, lw["sc_conv_w"], lw["cf_conv_w"], lw["cf_conv_b"], lw["cf_ln_g"],
            lw["cf_ln_b"], lw["w_o"], lw["norm2_g"], lw["router_wt"]]
    out_specs = [row(d),
                 pl.BlockSpec((tm * BF16_SUBLANES, LANES), lambda i: (i, 0)),
                 pl.BlockSpec((1, N_EXPERTS, tm), lambda i: (i // tiles_per_route, 0, i % tiles_per_route))]
    out_shape = [jax.ShapeDtypeStruct((t, d), F32),
                 jax.ShapeDtypeStruct((t * BF16_SUBLANES, LANES), BF16),
                 jax.ShapeDtypeStruct((route_batch, N_EXPERTS, route_len), F32)]
    return pl.pallas_call(
        functools.partial(_mixout_kernel, tm, tiles_per_seq, d_sc, d_cf),
        grid=(nt,), in_specs=in_specs, out_specs=out_specs, out_shape=out_shape,
        scratch_shapes=[pltpu.VMEM((tm + 2 * HALO, d_sc), F32), pltpu.VMEM((tm + 2 * HALO, d_cf), F32),
                        pltpu.VMEM((tm * BF16_SUBLANES, LANES), F32)],
        compiler_params=_params(("parallel",)), name="mixout",
    )(*args)


def _route_kernel(seq, cap, jb, lg_ref, idx_ref, gate_ref, slot_scr, aff_scr):
    lg = lg_ref[0]
    ex = jnp.exp(lg - jnp.max(lg, axis=0, keepdims=True))
    aff = ex / jnp.sum(ex, axis=0, keepdims=True)
    bits = pltpu.bitcast(aff, jnp.int32)
    tok = lax.broadcasted_iota(jnp.int32, aff.shape, 1)
    capf = jnp.float32(cap)

    def count(mask):
        return jnp.sum(jnp.where(mask, 1.0, 0.0), axis=1, keepdims=True)

    thr = jnp.zeros((N_EXPERTS, 1), jnp.int32)
    for bit in range(30, -1, -1):
        cand = thr | jnp.int32(1 << bit)
        thr = jnp.where(count(bits >= cand) >= capf, cand, thr)
    gt = bits > thr
    eq = bits == thr
    need = capf - count(gt)
    cut = jnp.zeros((N_EXPERTS, 1), jnp.int32)
    for bit in range(max(seq.bit_length(), 1) - 1, -1, -1):
        cand = cut | jnp.int32(1 << bit)
        cut = jnp.where(count(eq & (tok < cand)) <= need, cand, cut)
    sel = gt | (eq & (tok < cut))

    r_i = lax.broadcasted_iota(jnp.int32, (LANES, LANES), 0)
    c_i = lax.broadcasted_iota(jnp.int32, (LANES, LANES), 1)
    tri = jnp.where(r_i < c_i, 1.0, 0.0).astype(BF16)
    self = jnp.where(sel, 1.0, 0.0)
    carry = jnp.zeros((N_EXPERTS, 1), F32)
    slots = []
    for blk in range(seq // LANES):
        seg = self[:, blk * LANES:(blk + 1) * LANES]
        excl = jnp.dot(seg.astype(BF16), tri, preferred_element_type=F32)
        slots.append(jnp.where(seg > 0.0, excl + carry, -1.0))
        carry = carry + jnp.sum(seg, axis=1, keepdims=True)
    slot_scr[...] = jnp.concatenate(slots, axis=1)
    aff_scr[...] = aff

    n_chunks = seq // LANES
    jcol = lax.broadcasted_iota(jnp.int32, (jb, LANES), 0).astype(F32)
    lane = lax.broadcasted_iota(jnp.int32, (jb, LANES), 1).astype(F32)

    for e in range(N_EXPERTS):
        def per_block(n, _, e=e):
            j0 = pl.multiple_of(n * jb, jb)
            want = jcol + j0.astype(F32)

            def per_chunk(cb, carry):
                idx_acc, g_acc = carry
                off = pl.multiple_of(cb * LANES, LANES)
                hit = slot_scr[e:e + 1, pl.ds(off, LANES)] == want
                idx_acc = idx_acc + jnp.where(hit, lane + off.astype(F32), 0.0)
                g_acc = g_acc + jnp.where(hit, aff_scr[e:e + 1, pl.ds(off, LANES)], 0.0)
                return idx_acc, g_acc

            zero = jnp.zeros((jb, LANES), F32)
            idx_acc, g_acc = lax.fori_loop(0, n_chunks, per_chunk, (zero, zero), unroll=min(n_chunks, 4))
            idx_ref[0, e, pl.ds(j0, jb), :] = jnp.sum(idx_acc, axis=1, keepdims=True).astype(jnp.int32)
            gate_ref[0, e, pl.ds(j0, jb), :] = jnp.sum(g_acc, axis=1, keepdims=True)
            return 0

        lax.fori_loop(0, cap // jb, per_block, 0)


def _route(logits_t, cap):
    b, e, seq = logits_t.shape
    jb = min(cap, 64)
    return pl.pallas_call(
        functools.partial(_route_kernel, seq, cap, jb),
        grid=(b,),
        in_specs=[pl.BlockSpec((1, e, seq), lambda i: (i, 0, 0))],
        out_specs=[pl.BlockSpec((1, e, cap, 1), lambda i: (i, 0, 0, 0)),
                   pl.BlockSpec((1, e, cap, 1), lambda i: (i, 0, 0, 0))],
        out_shape=[jax.ShapeDtypeStruct((b, e, cap, 1), jnp.int32), jax.ShapeDtypeStruct((b, e, cap, 1), F32)],
        scratch_shapes=[pltpu.VMEM((e, seq), F32), pltpu.VMEM((e, seq), F32)],
        compiler_params=_params(("parallel",)), name="route",
    )(logits_t)


def _moe_up_kernel(cap, d_model, idx_ref, hg_ref, wg_ref, wu_ref, hid_ref, xg_scr, xs_scr):
    b, e, f = pl.program_id(0), pl.program_id(1), pl.program_id(2)

    @pl.when(f == 0)
    def _():
        base = (b * N_EXPERTS + e) * cap

        def copy(j, _):
            t = idx_ref[base + j]
            src = pl.multiple_of(t * BF16_SUBLANES, BF16_SUBLANES)
            dst = pl.multiple_of(j * BF16_SUBLANES, BF16_SUBLANES)
            xg_scr[pl.ds(dst, BF16_SUBLANES), :] = hg_ref[0, pl.ds(src, BF16_SUBLANES), :].astype(F32)
            return 0

        lax.fori_loop(0, cap, copy, 0, unroll=8)
        for c in range(d_model // LANES):
            xs_scr[:, c * LANES:(c + 1) * LANES] = xg_scr[pl.ds(c, cap, stride=BF16_SUBLANES), :].astype(BF16)

    xs = xs_scr[...]
    g = jnp.dot(xs, wg_ref[0], preferred_element_type=F32)
    u = jnp.dot(xs, wu_ref[0], preferred_element_type=F32)
    hid_ref[0, 0] = (g * _sigmoid(g) * u).astype(BF16)


def _moe_up(idx_flat, hg, wg, wu, *, cap, tf):
    b, rows, _ = hg.shape
    n_e, d_model, d_ff = wg.shape
    grid_spec = pltpu.PrefetchScalarGridSpec(
        num_scalar_prefetch=1, grid=(b, n_e, d_ff // tf),
        in_specs=[pl.BlockSpec((1, rows, LANES), lambda i, e, f, idx: (i, 0, 0), pipeline_mode=pl.Buffered(1)),
                  pl.BlockSpec((1, d_model, tf), lambda i, e, f, idx: (e, 0, f)),
                  pl.BlockSpec((1, d_model, tf), lambda i, e, f, idx: (e, 0, f))],
        out_specs=pl.BlockSpec((1, 1, cap, tf), lambda i, e, f, idx: (i, e, 0, f)),
        scratch_shapes=[pltpu.VMEM((cap * BF16_SUBLANES, LANES), F32), pltpu.VMEM((cap, d_model), BF16)])
    return pl.pallas_call(
        functools.partial(_moe_up_kernel, cap, d_model),
        grid_spec=grid_spec,
        out_shape=jax.ShapeDtypeStruct((b, n_e, cap, d_ff), BF16),
        compiler_params=_params(("arbitrary", "arbitrary", "arbitrary")), name="moe_up",
    )(idx_flat, hg, wg, wu)


def _moe_down_kernel(cap, half, idx_ref, hid_ref, gate_ref, wd_ref, acc_ref, og_scr):
    b, e = pl.program_id(0), pl.program_id(2)

    @pl.when(e == 0)
    def _():
        acc_ref[...] = jnp.zeros_like(acc_ref)

    o = jnp.dot(hid_ref[0, 0], wd_ref[0], preferred_element_type=F32) * gate_ref[0]
    for c in range(half // LANES):
        og_scr[pl.ds(c, cap, stride=SUBLANES), :] = o[:, c * LANES:(c + 1) * LANES]
    base = (b * N_EXPERTS + e) * cap

    def add(j, _):
        t = idx_ref[base + j]
        dst = pl.multiple_of(t * SUBLANES, SUBLANES)
        src = pl.multiple_of(j * SUBLANES, SUBLANES)
        acc_ref[0, 0, pl.ds(dst, SUBLANES), :] = acc_ref[0, 0, pl.ds(dst, SUBLANES), :] + og_scr[pl.ds(src, SUBLANES), :]
        return 0

    lax.fori_loop(0, cap, add, 0, unroll=8)


def _moe_down(idx_flat, hid, gate, wd, *, seq):
    b, n_e, cap, d_ff = hid.shape
    d_model = wd.shape[2]
    half = d_model // 2
    grid_spec = pltpu.PrefetchScalarGridSpec(
        num_scalar_prefetch=1, grid=(b, 2, n_e),
        in_specs=[pl.BlockSpec((1, 1, cap, d_ff), lambda i, p, e, idx: (i, e, 0, 0)),
                  pl.BlockSpec((1, cap, 1), lambda i, p, e, idx: (i * n_e + e, 0, 0)),
                  pl.BlockSpec((1, d_ff, half), lambda i, p, e, idx: (e, 0, p))],
        out_specs=pl.BlockSpec((1, 1, seq * SUBLANES, LANES), lambda i, p, e, idx: (i, p, 0, 0)),
        scratch_shapes=[pltpu.VMEM((cap * SUBLANES, LANES), F32)])
    return pl.pallas_call(
        functools.partial(_moe_down_kernel, cap, half),
        grid_spec=grid_spec,
        out_shape=jax.ShapeDtypeStruct((b, 2, seq * SUBLANES, LANES), F32),
        compiler_params=_params(("arbitrary", "arbitrary", "arbitrary")), name="moe_down",
    )(idx_flat, hid, gate, wd)


def _final_kernel(tm, x_ref, moe_ref, g2_ref, g_ref, y_ref):
    x = _moe_residual(x_ref[...], moe_ref, g2_ref[0], tm)
    y_ref[...] = _rms(x, g_ref[...])


def _final(x, moe, mod_prev, g, *, tm, tiles_per_mod):
    t, d = x.shape
    tiles_per_row = moe.shape[2] // (tm * SUBLANES)
    return pl.pallas_call(
        functools.partial(_final_kernel, tm),
        grid=(t // tm,),
        in_specs=[pl.BlockSpec((tm, d), lambda i: (i, 0)),
                  pl.BlockSpec((1, 2, tm * SUBLANES, LANES), lambda i: (i // tiles_per_row, 0, i % tiles_per_row, 0)),
                  pl.BlockSpec((1, 1, d), lambda i: (i // tiles_per_mod, 0, 5)),
                  pl.BlockSpec(g.shape, lambda i: (0, 0))],
        out_specs=pl.BlockSpec((tm, d), lambda i: (i, 0)),
        out_shape=jax.ShapeDtypeStruct((t, d), F32),
        compiler_params=_params(("parallel",)), name="final_norm",
    )(x, moe, mod_prev, g)


def _rope_swap_perm():
    q = QK_ROPE // 4
    return jnp.array(list(range(q, 2 * q)) + list(range(0, q)) + list(range(3 * q, 4 * q)) + list(range(2 * q, 3 * q)),
                     jnp.int32)


def _rope_table(n_tokens):
    rows = n_tokens // GRID_W
    row = jnp.repeat(jnp.arange(rows, dtype=F32), GRID_W)
    col = jnp.tile(jnp.arange(GRID_W, dtype=F32), rows)
    n_pairs = QK_ROPE // 4
    inv = ROPE_BASE ** (-jnp.arange(n_pairs, dtype=F32) / n_pairs)
    ang_r = row[:, None] * inv
    ang_c = col[:, None] * inv
    cr, sr, cc, sn = jnp.cos(ang_r), jnp.sin(ang_r), jnp.cos(ang_c), jnp.sin(ang_c)
    return jnp.concatenate([cr, cr, cc, cc, -sr, sr, -sn, sn], axis=1)


def _layer_weights(l, w):
    d = w["w_in"].shape[1]
    q_lora = w["q_norm_g"].shape[1]
    kv_lora = w["kv_norm_g"].shape[1]
    perm = _rope_swap_perm()
    w_in = w["w_in"][l]
    c1 = q_lora + kv_lora
    kr_w = w_in[:, c1:c1 + QK_ROPE]
    w_in_p = jnp.concatenate([w_in[:, :c1 + QK_ROPE], kr_w[:, perm], w_in[:, c1 + QK_ROPE:]], axis=1).astype(BF16)
    w_uq = w["w_uq"][l]
    rope_w = w_uq[:, :, QK_NOPE:]
    w_uq_p = jnp.concatenate([w_uq, rope_w[:, :, perm]], axis=2).reshape(q_lora, N_HEADS * HEAD_SLAB).astype(BF16)
    w_ukv = w["w_ukv"][l]
    w_ukv_p = jnp.concatenate([w_ukv[:, :, :QK_NOPE].reshape(kv_lora, -1), w_ukv[:, :, QK_NOPE:].reshape(kv_lora, -1)],
                              axis=1).astype(BF16)
    return dict(
        w_in=w_in_p, q_norm_g=w["q_norm_g"][l][None], w_uq=w_uq_p, kv_norm_g=w["kv_norm_g"][l][None], w_ukv=w_ukv_p,
        norm1_g=w["norm1_g"][l][None], norm2_g=w["norm2_g"][l][None],
        sc_conv_w=w["sc_conv_w"][l], cf_conv_w=w["cf_conv_w"][l], cf_conv_b=w["cf_conv_b"][l][None],
        cf_ln_g=w["cf_ln_g"][l][None], cf_ln_b=w["cf_ln_b"][l][None],
        w_o=w["w_o"][l].astype(BF16), router_wt=w["router_w"][l].T,
        wg=w["exp_w_gate"][l].astype(BF16), wu=w["exp_w_up"][l].astype(BF16), wd=w["exp_w_down"][l].astype(BF16))


def _run_stream(x0, mods, lws, final_g, *, batch, seq, tab, cache_k, cache_v, merge_moe_rows, tm, tq, tk, tf):
    t, d = x0.shape
    n_mod = mods[0].shape[0]
    tiles_per_mod = t // n_mod // tm
    tiles_per_seq = seq // tm
    cap = max(1, EC_CAPACITY * seq // N_EXPERTS)
    moe_b, moe_seq = (1, t) if merge_moe_rows else (batch, seq)
    moe_cap = cap * (batch // moe_b)
    x, moe = x0, None
    ckvs, krs = [], []
    for l, lw in enumerate(lws):
        x, q, k, v, cv, ckvn, kr = _inproj(x, moe, mods[l - 1] if l else None, mods[l], lw["norm1_g"], tab, lw,
                                           tm=tm, tiles_per_mod=tiles_per_mod, tiles_per_seq=tiles_per_seq)
        ckvs.append(ckvn)
        krs.append(kr)
        kc = None if cache_k is None else cache_k[l]
        vc = None if cache_v is None else cache_v[l]
        o_att = _attention(q, k, v, kc, vc, batch=batch, seq=seq, tq=tq, tk=tk)
        x, hg, logits_t = _mixout(o_att, cv, x, mods[l], lw, tm=tm, tiles_per_mod=tiles_per_mod,
                                  tiles_per_seq=tiles_per_seq, route_batch=batch, route_len=seq)
        idx, gate = _route(logits_t, cap)
        if moe_b != batch:
            offs = (jnp.arange(batch, dtype=jnp.int32) * seq)[:, None, None, None]
            idx = jnp.transpose(idx + offs, (1, 0, 2, 3)).reshape(1, N_EXPERTS, moe_cap, 1)
            gate = jnp.transpose(gate, (1, 0, 2, 3)).reshape(1, N_EXPERTS, moe_cap, 1)
        idx_flat = idx.reshape(-1)
        gate = gate.reshape(moe_b * N_EXPERTS, moe_cap, 1)
        hid = _moe_up(idx_flat, hg.reshape(moe_b, moe_seq * BF16_SUBLANES, LANES), lw["wg"], lw["wu"], cap=moe_cap, tf=tf)
        moe = _moe_down(idx_flat, hid, gate, lw["wd"], seq=moe_seq)
    y = _final(x, moe, mods[-1], final_g, tm=tm, tiles_per_mod=tiles_per_mod)
    return y, ckvs, krs


def kernel(x_prompt, x_sample, cache_ckv, cache_krope, c, c_ctx, ada_w, ada_b, norm1_g, norm2_g, w_in, q_norm_g, w_uq,
           kv_norm_g, w_ukv, sc_conv_w, cf_conv_w, cf_conv_b, cf_ln_g, cf_ln_b, w_o, router_w, exp_w_gate, exp_w_up,
           exp_w_down, final_norm_g):
    batch, seq, d = x_prompt.shape
    dec_batch, dec_seq, _ = x_sample.shape
    depth = ada_w.shape[0]
    past = cache_ckv.shape[2]
    w = dict(norm1_g=norm1_g, norm2_g=norm2_g, w_in=w_in, q_norm_g=q_norm_g, w_uq=w_uq, kv_norm_g=kv_norm_g, w_ukv=w_ukv,
             sc_conv_w=sc_conv_w, cf_conv_w=cf_conv_w, cf_conv_b=cf_conv_b, cf_ln_g=cf_ln_g, cf_ln_b=cf_ln_b, w_o=w_o,
             router_w=router_w, exp_w_gate=exp_w_gate, exp_w_up=exp_w_up, exp_w_down=exp_w_down)
    lws = [_layer_weights(l, w) for l in range(depth)]

    n_rows = -(-(1 + dec_batch) // SUBLANES) * SUBLANES
    c_all = jnp.zeros((n_rows, d), F32).at[0].set(c_ctx).at[1:1 + dec_batch].set(c)
    mod = _ada_mod(c_all, ada_w, ada_b)
    mods_p = [mod[l, 0:1][:, None, :] for l in range(depth)]
    mods_s = [mod[l, 1:1 + dec_batch][:, None, :] for l in range(depth)]

    tm = 256
    final_g = final_norm_g[None]

    ident = jnp.concatenate([jnp.ones((tm, QK_ROPE), F32), jnp.zeros((tm, QK_ROPE), F32)], axis=1)
    y_p, ckvs, krs = _run_stream(x_prompt.reshape(batch * seq, d), mods_p, lws, final_g, batch=batch, seq=seq, tab=ident,
                                 cache_k=None, cache_v=None, merge_moe_rows=True, tm=tm, tq=min(seq, 256), tk=min(seq, 512), tf=512)

    ckv_c = jnp.transpose(cache_ckv, (1, 0, 2, 3)).reshape(depth, dec_batch * past, -1)
    kr_c = jnp.transpose(cache_krope, (1, 0, 2, 3)).reshape(depth, dec_batch * past, -1)
    kr_c = jnp.pad(kr_c, ((0, 0), (0, 0), (0, LANES - kr_c.shape[-1])))
    w_ukv_all = jnp.stack([lw["w_ukv"] for lw in lws])
    ck, cvv = _cache_kv(ckv_c, kr_c, w_ukv_all, tm=min(256, dec_batch * past))
    ck = ck.reshape(depth, dec_batch, past, -1)
    cvv = cvv.reshape(depth, dec_batch, past, -1)
    y_s, _, _ = _run_stream(x_sample.reshape(dec_batch * dec_seq, d), mods_s, lws, final_g, batch=dec_batch, seq=dec_seq,
                            tab=_rope_table(dec_seq), cache_k=ck, cache_v=cvv, merge_moe_rows=False, tm=tm, tq=256, tk=512, tf=512)

    new_ckv = jnp.stack([a.reshape(batch, seq, -1) for a in ckvs], axis=1)
    new_kr = jnp.stack([a.reshape(batch, seq, -1) for a in krs], axis=1)
    return (y_p.reshape(batch, seq, d), y_s.reshape(dec_batch, dec_seq, d), new_ckv, new_kr)
```

```python
import functools

import jax
import jax.numpy as jnp
from jax import lax
from jax.experimental import pallas as pl
from jax.experimental.pallas import tpu as pltpu

LANES = 128
SUBLANES = 8
BF16_SUBLANES = 16
VMEM_LIMIT_BYTES = 56 * 1024 * 1024

N_HEADS = 8
QK_NOPE = 128
QK_ROPE = 64
V_DIM = 128
HEAD_SLAB = 256
N_EXPERTS = 16
EC_CAPACITY = 2
GRID_W = 64
SC_WIDTH = 3
CF_WIDTH = 31
CF_PAD = (CF_WIDTH - 1) // 2
HALO = 16
ROPE_BASE = 10000.0
NORM_EPS = 1e-6
ATTN_SCALE = (QK_NOPE + QK_ROPE) ** -0.5
LOG2_E = 1.4426950408889634
Q_SCALE = ATTN_SCALE * LOG2_E

F32 = jnp.float32
BF16 = jnp.bfloat16


def _params(sem):
    return pltpu.CompilerParams(dimension_semantics=sem, vmem_limit_bytes=VMEM_LIMIT_BYTES)


def _sigmoid(x):
    return 1.0 / (1.0 + jnp.exp(-x))


def _rms(x, g):
    return x * lax.rsqrt(jnp.mean(x * x, axis=-1, keepdims=True) + NORM_EPS) * g


def _split3_dot(a, b, dims):
    a_hi = a.astype(BF16)
    a_lo = (a - a_hi.astype(F32)).astype(BF16)
    b_hi = b.astype(BF16)
    b_lo = (b - b_hi.astype(F32)).astype(BF16)
    dot = functools.partial(lax.dot_general, dimension_numbers=(dims, ((), ())), preferred_element_type=F32)
    return dot(a_hi, b_hi) + dot(a_lo, b_hi) + dot(a_hi, b_lo)


def _ada_kernel(c_ref, w_ref, b_ref, o_ref):
    cv = c_ref[...]
    s = cv * _sigmoid(cv)
    o_ref[0] = _split3_dot(s, w_ref[0], ((1,), (0,))) + b_ref[0]


def _ada_mod(c_all, ada_w, ada_b):
    depth, d, n6 = ada_w.shape
    rows = c_all.shape[0]
    tn = 1024
    return pl.pallas_call(
        _ada_kernel,
        grid=(depth, n6 // tn),
        in_specs=[pl.BlockSpec((rows, d), lambda l, j: (0, 0)),
                  pl.BlockSpec((1, d, tn), lambda l, j: (l, 0, j)),
                  pl.BlockSpec((1, 1, tn), lambda l, j: (l, 0, j))],
        out_specs=pl.BlockSpec((1, rows, tn), lambda l, j: (l, 0, j)),
        out_shape=jax.ShapeDtypeStruct((depth, rows, n6), F32),
        compiler_params=_params(("parallel", "parallel")),
        name="ada_mod",
    )(c_all, ada_w, ada_b.reshape(depth, 1, n6))


def _moe_residual(x, moe_ref, g2, tm):
    chunks = []
    for p in range(2):
        for c in range(SUBLANES):
            chunks.append(moe_ref[0, p, pl.ds(c, tm, stride=SUBLANES), :])
    return x + g2 * jnp.concatenate(chunks, axis=1)


def _rope_pair(v, tab):
    prod = v * tab
    r = prod + pltpu.roll(prod, QK_ROPE, 1)
    lane = lax.broadcasted_iota(jnp.int32, r.shape, 1)
    return jnp.where(lane < QK_ROPE, r, 0.0)


def _inproj_kernel(has_moe, tm, d_model, q_lora, kv_lora, *refs):
    if has_moe:
        (x_ref, moe_ref, g2_ref, shsc_ref, n1_ref, tab_ref, win_ref, qg_ref, wuq_ref, kvg_ref, wuk_ref, wvt_ref,
         xo_ref, q_ref, k_ref, vt_ref, cv_ref, ckv_ref, kr_ref) = refs
        x = _moe_residual(x_ref[...], moe_ref, g2_ref[0], tm)
        xo_ref[...] = x
    else:
        (x_ref, shsc_ref, n1_ref, tab_ref, win_ref, qg_ref, wuq_ref, kvg_ref, wuk_ref, wvt_ref,
         q_ref, k_ref, vt_ref, cv_ref, ckv_ref, kr_ref) = refs
        x = x_ref[...]
    sh1 = shsc_ref[0, :, :d_model]
    sc1 = shsc_ref[0, :, d_model:]
    h = _rms(x, n1_ref[...]) * (1.0 + sc1) + sh1
    u = jnp.dot(h.astype(BF16), win_ref[...], preferred_element_type=F32)
    tab = tab_ref[...]

    cqn = _rms(u[:, :q_lora], qg_ref[...])
    qf = jnp.dot(cqn.astype(BF16), wuq_ref[...], preferred_element_type=F32)
    for hd in range(N_HEADS):
        o = hd * HEAD_SLAB
        q_ref[:, o:o + QK_NOPE] = (qf[:, o:o + QK_NOPE] * Q_SCALE).astype(BF16)
        q_ref[:, o + QK_NOPE:o + HEAD_SLAB] = (_rope_pair(qf[:, o + QK_NOPE:o + HEAD_SLAB], tab) * Q_SCALE).astype(BF16)

    c0 = q_lora
    ckvn = _rms(u[:, c0:c0 + kv_lora], kvg_ref[...])
    ckv_ref[...] = ckvn
    ckvb = ckvn.astype(BF16)
    knf = jnp.dot(ckvb, wuk_ref[...], preferred_element_type=F32)
    vt_ref[0] = lax.dot_general(wvt_ref[...], ckvb, (((1,), (1,)), ((), ())), preferred_element_type=F32).astype(BF16)
    c1 = c0 + kv_lora
    krv = u[:, c1:c1 + LANES]
    kr_ref[...] = krv[:, :QK_ROPE]
    kr2 = _rope_pair(krv, tab).astype(BF16)
    for hd in range(N_HEADS):
        o = hd * HEAD_SLAB
        k_ref[:, o:o + QK_NOPE] = knf[:, hd * QK_NOPE:(hd + 1) * QK_NOPE].astype(BF16)
        k_ref[:, o + QK_NOPE:o + HEAD_SLAB] = kr2
    cv_ref[...] = u[:, c1 + LANES:]


def _inproj(x, moe, mod_prev, mod, n1, tab, lw, *, tm, tiles_per_mod, tiles_per_seq):
    t, d = x.shape
    nt = t // tm
    p_in = lw["w_in"].shape[1]
    q_lora = lw["q_norm_g"].shape[1]
    kv_lora = lw["kv_norm_g"].shape[1]
    n_cv = p_in - q_lora - kv_lora - LANES
    has_moe = moe is not None
    tiles_per_row = None if moe is None else moe.shape[2] // (tm * SUBLANES)

    full = lambda a: pl.BlockSpec(a.shape, lambda i: (0,) * a.ndim)
    row = lambda w: pl.BlockSpec((tm, w), lambda i: (i, 0))
    in_specs = [row(d)]
    args = [x]
    if has_moe:
        in_specs += [pl.BlockSpec((1, 2, tm * SUBLANES, LANES), lambda i: (i // tiles_per_row, 0, i % tiles_per_row, 0)),
                     pl.BlockSpec((1, 1, d), lambda i: (i // tiles_per_mod, 0, 5))]
        args += [moe, mod_prev]
    in_specs += [pl.BlockSpec((1, 1, 2 * d), lambda i: (i // tiles_per_mod, 0, 0)),
                 full(n1),
                 pl.BlockSpec((tm, LANES), lambda i: (i % tiles_per_seq, 0)),
                 full(lw["w_in"]), full(lw["q_norm_g"]), full(lw["w_uq"]), full(lw["kv_norm_g"]), full(lw["w_uk"]), full(lw["w_vt"])]
    args += [mod, n1, tab, lw["w_in"], lw["q_norm_g"], lw["w_uq"], lw["kv_norm_g"], lw["w_uk"], lw["w_vt"]]

    out_specs, out_shape = [], []
    if has_moe:
        out_specs.append(row(d))
        out_shape.append(jax.ShapeDtypeStruct((t, d), F32))
    seq = tiles_per_seq * tm
    for w, dt in ((N_HEADS * HEAD_SLAB, BF16), (N_HEADS * HEAD_SLAB, BF16), (None, BF16),
                  (n_cv, F32), (kv_lora, F32), (QK_ROPE, F32)):
        if w is None:
            out_specs.append(pl.BlockSpec((1, N_HEADS * V_DIM, tm), lambda i: (i // tiles_per_seq, 0, i % tiles_per_seq)))
            out_shape.append(jax.ShapeDtypeStruct((t // seq, N_HEADS * V_DIM, seq), dt))
        else:
            out_specs.append(row(w))
            out_shape.append(jax.ShapeDtypeStruct((t, w), dt))

    outs = pl.pallas_call(
        functools.partial(_inproj_kernel, has_moe, tm, d, q_lora, kv_lora),
        grid=(nt,), in_specs=in_specs, out_specs=out_specs, out_shape=out_shape,
        compiler_params=_params(("parallel",)), name="inproj",
    )(*args)
    if not has_moe:
        outs = [x] + list(outs)
    return outs


def _cachekv_kernel(ckv_ref, kr_ref, wuk_ref, wvt_ref, k_ref, vt_ref):
    ckvb = ckv_ref[0, 0].astype(BF16)
    knf = jnp.dot(ckvb, wuk_ref[0], preferred_element_type=F32)
    kr2 = kr_ref[0, 0].astype(BF16)
    for hd in range(N_HEADS):
        o = hd * HEAD_SLAB
        k_ref[0, 0, :, o:o + QK_NOPE] = knf[:, hd * QK_NOPE:(hd + 1) * QK_NOPE].astype(BF16)
        k_ref[0, 0, :, o + QK_NOPE:o + HEAD_SLAB] = kr2
    vt_ref[0, 0] = lax.dot_general(wvt_ref[0], ckvb, (((1,), (1,)), ((), ())), preferred_element_type=F32).astype(BF16)


def _cache_kv(ckv_c, kr_c, w_uk_all, w_vt_all):
    depth, b, past, kv_lora = ckv_c.shape
    nk = N_HEADS * HEAD_SLAB
    nv = N_HEADS * V_DIM
    return pl.pallas_call(
        _cachekv_kernel,
        grid=(depth, b),
        in_specs=[pl.BlockSpec((1, 1, past, kv_lora), lambda l, i: (l, i, 0, 0)),
                  pl.BlockSpec((1, 1, past, LANES), lambda l, i: (l, i, 0, 0)),
                  pl.BlockSpec((1, kv_lora, N_HEADS * QK_NOPE), lambda l, i: (l, 0, 0)),
                  pl.BlockSpec((1, nv, kv_lora), lambda l, i: (l, 0, 0))],
        out_specs=[pl.BlockSpec((1, 1, past, nk), lambda l, i: (l, i, 0, 0)),
                   pl.BlockSpec((1, 1, nv, past), lambda l, i: (l, i, 0, 0))],
        out_shape=[jax.ShapeDtypeStruct((depth, b, past, nk), BF16), jax.ShapeDtypeStruct((depth, b, nv, past), BF16)],
        compiler_params=_params(("parallel", "parallel")), name="cache_kv",
    )(ckv_c, kr_c, w_uk_all, w_vt_all)


def _attn_kernel(has_ctx, n_hd, lk, tk, *refs):
    if has_ctx:
        q_ref, k_ref, vt_ref, kc_ref, vct_ref, o_ref, s_scr, sc_scr = refs
    else:
        q_ref, k_ref, vt_ref, o_ref, s_scr = refs
    tq = q_ref.shape[0]
    n = lk // tk
    qs = [q_ref[:, g * HEAD_SLAB:(g + 1) * HEAD_SLAB] for g in range(n_hd)]

    def scores(g, kc, s_ref):
        s = lax.dot_general(kc, qs[g], (((1,), (1,)), ((), ())), preferred_element_type=F32)
        s_ref[...] = s
        return jnp.max(s, axis=0, keepdims=True)

    def k_chunk(g, c):
        return k_ref[c * tk:(c + 1) * tk, g * HEAD_SLAB:(g + 1) * HEAD_SLAB]

    def vt_chunk(g, c):
        return vt_ref[0, g * V_DIM:(g + 1) * V_DIM, c * tk:(c + 1) * tk]

    def accumulate(state, s_ref, smax, vt):
        m, l, acc = state
        m_new = jnp.maximum(m, smax)
        alpha = jnp.exp2(m - m_new)
        p = jnp.exp2(s_ref[...] - m_new)
        l = alpha * l + jnp.sum(p, axis=0, keepdims=True)
        acc = alpha * acc + jnp.dot(vt, p.astype(BF16), preferred_element_type=F32)
        return m_new, l, acc

    states = [(jnp.full((1, tq), -jnp.inf, F32), jnp.zeros((1, tq), F32), jnp.zeros((V_DIM, tq), F32))] * n_hd
    smax = [scores(g, k_chunk(g, 0), s_scr.at[g, 0]) for g in range(n_hd)]
    for c in range(n):
        for g in range(n_hd):
            smax_next = None
            if c + 1 < n:
                smax_next = scores(g, k_chunk(g, c + 1), s_scr.at[g, (c + 1) % 2])
            elif has_ctx:
                smax_next = scores(g, kc_ref[0, :, g * HEAD_SLAB:(g + 1) * HEAD_SLAB], sc_scr.at[g])
            states[g] = accumulate(states[g], s_scr.at[g, c % 2], smax[g], vt_chunk(g, c))
            smax[g] = smax_next
    for g in range(n_hd):
        state = states[g]
        if has_ctx:
            state = accumulate(state, sc_scr.at[g], smax[g], vct_ref[0, g * V_DIM:(g + 1) * V_DIM, :])
        _, l, acc = state
        o_ref[:, g * V_DIM:(g + 1) * V_DIM] = jnp.transpose(acc / l).astype(BF16)


def _attention(q, k, vt, kc, vct, *, batch, seq, tq, tk, n_hd):
    t = q.shape[0]
    nq = seq // tq
    has_ctx = kc is not None
    in_specs = [pl.BlockSpec((tq, n_hd * HEAD_SLAB), lambda b, h, i: (b * nq + i, h)),
                pl.BlockSpec((seq, n_hd * HEAD_SLAB), lambda b, h, i: (b, h)),
                pl.BlockSpec((1, n_hd * V_DIM, seq), lambda b, h, i: (b, h, 0))]
    args = [q, k, vt]
    scratch = [pltpu.VMEM((n_hd, 2, tk, tq), F32)]
    if has_ctx:
        past = kc.shape[1]
        in_specs += [pl.BlockSpec((1, past, n_hd * HEAD_SLAB), lambda b, h, i: (b, 0, h)),
                     pl.BlockSpec((1, n_hd * V_DIM, past), lambda b, h, i: (b, h, 0))]
        args += [kc, vct]
        scratch += [pltpu.VMEM((n_hd, past, tq), F32)]
    return pl.pallas_call(
        functools.partial(_attn_kernel, has_ctx, n_hd, seq, tk),
        grid=(batch, N_HEADS // n_hd, nq), in_specs=in_specs,
        out_specs=pl.BlockSpec((tq, n_hd * V_DIM), lambda b, h, i: (b * nq + i, h)),
        out_shape=jax.ShapeDtypeStruct((t, N_HEADS * V_DIM), BF16),
        scratch_shapes=scratch,
        compiler_params=_params(("parallel", "parallel", "arbitrary")), name="attention",
    )(*args)


def _mixout_kernel(tm, tiles_per_seq, d_sc, d_cf,
                   o_ref, cv_ref, cvp_ref, cvn_ref, x_ref, g1_ref, sh2_ref, sc2_ref,
                   scw_ref, cfw_ref, cfb_ref, lng_ref, lnb_ref, wo_ref, n2_ref, rwt_ref,
                   xo_ref, hg_ref, lg_ref, ext_sc, ext_cf, hg_scr):
    i = pl.program_id(0)
    first = (i % tiles_per_seq) == 0
    last = (i % tiles_per_seq) == tiles_per_seq - 1
    o_scx, o_scb, o_scc, o_cfa, o_cfg = 0, d_sc, 2 * d_sc, 3 * d_sc, 3 * d_sc + d_cf

    def sc_in(ref):
        return ref[:, o_scc:o_scc + d_sc] * ref[:, o_scx:o_scx + d_sc]

    def cf_in(ref):
        return ref[:, o_cfa:o_cfa + d_cf] * _sigmoid(ref[:, o_cfg:o_cfg + d_cf])

    ext_sc[0:HALO, :] = jnp.where(first, 0.0, sc_in(cvp_ref))
    ext_sc[HALO:HALO + tm, :] = sc_in(cv_ref)
    ext_sc[HALO + tm:, :] = jnp.where(last, 0.0, sc_in(cvn_ref))
    ext_cf[0:HALO, :] = jnp.where(first, 0.0, cf_in(cvp_ref))
    ext_cf[HALO:HALO + tm, :] = cf_in(cv_ref)
    ext_cf[HALO + tm:, :] = jnp.where(last, 0.0, cf_in(cvn_ref))

    sc_pad = (SC_WIDTH - 1) // 2
    y_sc = jnp.zeros((tm, d_sc), F32)
    for kk in range(SC_WIDTH):
        y_sc = y_sc + scw_ref[kk:kk + 1, :] * ext_sc[pl.ds(HALO - sc_pad + kk, tm), :]
    y_sc = cv_ref[:, o_scb:o_scb + d_sc] * y_sc

    z = jnp.zeros((tm, d_cf), F32)
    for kk in range(CF_WIDTH):
        z = z + cfw_ref[kk:kk + 1, :] * ext_cf[pl.ds(HALO - CF_PAD + kk, tm), :]
    z = z + cfb_ref[...]
    mu = jnp.mean(z, axis=-1, keepdims=True)
    zc = z - mu
    var = jnp.mean(zc * zc, axis=-1, keepdims=True)
    z = zc * lax.rsqrt(var + NORM_EPS) * lng_ref[...] + lnb_ref[...]
    z = z * _sigmoid(z)

    mixed = jnp.concatenate([o_ref[...], y_sc.astype(BF16), z.astype(BF16)], axis=1)
    y = jnp.dot(mixed, wo_ref[...], preferred_element_type=F32)
    x = x_ref[...] + g1_ref[0] * y
    xo_ref[...] = x
    h2 = _rms(x, n2_ref[...]) * (1.0 + sc2_ref[0]) + sh2_ref[0]

    lg_ref[0] = _split3_dot(rwt_ref[...], h2, ((1,), (1,)))

    d_model = h2.shape[1]
    for c in range(d_model // LANES):
        hg_scr[pl.ds(c, tm, stride=BF16_SUBLANES), :] = h2[:, c * LANES:(c + 1) * LANES]
    hg_ref[...] = hg_scr[...].astype(BF16)


def _mixout(o_att, cv, x, mod, lw, *, tm, tiles_per_mod, tiles_per_seq, route_batch, route_len):
    t, d = x.shape
    nt = t // tm
    n_cv = cv.shape[1]
    d_sc = lw["sc_conv_w"].shape[1]
    d_cf = lw["cf_conv_w"].shape[1]
    hb = tm // HALO
    n_halo = t // HALO
    tiles_per_route = route_len // tm
    full = lambda a: pl.BlockSpec(a.shape, lambda i: (0,) * a.ndim)
    row = lambda w: pl.BlockSpec((tm, w), lambda i: (i, 0))
    modspec = lambda blk: pl.BlockSpec((1, 1, d), lambda i: (i // tiles_per_mod, 0, blk))
    in_specs = [row(o_att.shape[1]), row(n_cv),
                pl.BlockSpec((HALO, n_cv), lambda i: (jnp.maximum(i * hb - 1, 0), 0)),
                pl.BlockSpec((HALO, n_cv), lambda i: (jnp.minimum((i + 1) * hb, n_halo - 1), 0)),
                row(d), modspec(2), modspec(3), modspec(4),
                full(lw["sc_conv_w"]), full(lw["cf_conv_w"]), full(lw["cf_conv_b"]), full(lw["cf_ln_g"]),
                full(lw["cf_ln_b"]), full(lw["w_o"]), full(lw["norm2_g"]), full(lw["router_wt"])]
    args = [o_att, cv, cv, cv, x, mod, mod, mod, lw["sc_conv_w"], lw["cf_conv_w"], lw["cf_conv_b"], lw["cf_ln_g"],
            lw["cf_ln_b"], lw["w_o"], lw["norm2_g"], lw["router_wt"]]
    out_specs = [row(d),
                 pl.BlockSpec((tm * BF16_SUBLANES, LANES), lambda i: (i, 0)),
                 pl.BlockSpec((1, N_EXPERTS, tm), lambda i: (i // tiles_per_route, 0, i % tiles_per_route))]
    out_shape = [jax.ShapeDtypeStruct((t, d), F32),
                 jax.ShapeDtypeStruct((t * BF16_SUBLANES, LANES), BF16),
                 jax.ShapeDtypeStruct((route_batch, N_EXPERTS, route_len), F32)]
    return pl.pallas_call(
        functools.partial(_mixout_kernel, tm, tiles_per_seq, d_sc, d_cf),
        grid=(nt,), in_specs=in_specs, out_specs=out_specs, out_shape=out_shape,
        scratch_shapes=[pltpu.VMEM((tm + 2 * HALO, d_sc), F32), pltpu.VMEM((tm + 2 * HALO, d_cf), F32),
                        pltpu.VMEM((tm * BF16_SUBLANES, LANES), F32)],
        compiler_params=_params(("parallel",)), name="mixout",
    )(*args)


def _route_kernel(seq, cap, jb, lg_ref, idx_ref, gate_ref, slot_scr, aff_scr):
    lg = lg_ref[0]
    ex = jnp.exp(lg - jnp.max(lg, axis=0, keepdims=True))
    aff = ex / jnp.sum(ex, axis=0, keepdims=True)
    bits = pltpu.bitcast(aff, jnp.int32)
    tok = lax.broadcasted_iota(jnp.int32, aff.shape, 1)
    capf = jnp.float32(cap)

    def count(mask):
        return jnp.sum(jnp.where(mask, 1.0, 0.0), axis=1, keepdims=True)

    thr = jnp.zeros((N_EXPERTS, 1), jnp.int32)
    for bit in range(30, -1, -1):
        cand = thr | jnp.int32(1 << bit)
        thr = jnp.where(count(bits >= cand) >= capf, cand, thr)
    gt = bits > thr
    eq = bits == thr
    need = capf - count(gt)
    cut = jnp.zeros((N_EXPERTS, 1), jnp.int32)
    for bit in range(max(seq.bit_length(), 1) - 1, -1, -1):
        cand = cut | jnp.int32(1 << bit)
        cut = jnp.where(count(eq & (tok < cand)) <= need, cand, cut)
    sel = gt | (eq & (tok < cut))

    r_i = lax.broadcasted_iota(jnp.int32, (LANES, LANES), 0)
    c_i = lax.broadcasted_iota(jnp.int32, (LANES, LANES), 1)
    tri = jnp.where(r_i < c_i, 1.0, 0.0).astype(BF16)
    self = jnp.where(sel, 1.0, 0.0)
    carry = jnp.zeros((N_EXPERTS, 1), F32)
    slots = []
    for blk in range(seq // LANES):
        seg = self[:, blk * LANES:(blk + 1) * LANES]
        excl = jnp.dot(seg.astype(BF16), tri, preferred_element_type=F32)
        slots.append(jnp.where(seg > 0.0, excl + carry, -1.0))
        carry = carry + jnp.sum(seg, axis=1, keepdims=True)
    slot_scr[...] = jnp.concatenate(slots, axis=1)
    aff_scr[...] = aff

    n_chunks = seq // LANES
    jcol = lax.broadcasted_iota(jnp.int32, (jb, LANES), 0).astype(F32)
    lane = lax.broadcasted_iota(jnp.int32, (jb, LANES), 1).astype(F32)

    for e in range(N_EXPERTS):
        def per_block(n, _, e=e):
            j0 = pl.multiple_of(n * jb, jb)
            want = jcol + j0.astype(F32)

            def per_chunk(cb, carry):
                idx_acc, g_acc = carry
                off = pl.multiple_of(cb * LANES, LANES)
                hit = slot_scr[e:e + 1, pl.ds(off, LANES)] == want
                idx_acc = idx_acc + jnp.where(hit, lane + off.astype(F32), 0.0)
                g_acc = g_acc + jnp.where(hit, aff_scr[e:e + 1, pl.ds(off, LANES)], 0.0)
                return idx_acc, g_acc

            zero = jnp.zeros((jb, LANES), F32)
            idx_acc, g_acc = lax.fori_loop(0, n_chunks, per_chunk, (zero, zero), unroll=min(n_chunks, 4))
            idx_ref[0, e, pl.ds(j0, jb), :] = jnp.sum(idx_acc, axis=1, keepdims=True).astype(jnp.int32)
            gate_ref[0, e, pl.ds(j0, jb), :] = jnp.sum(g_acc, axis=1, keepdims=True)
            return 0

        lax.fori_loop(0, cap // jb, per_block, 0)


def _route(logits_t, cap):
    b, e, seq = logits_t.shape
    jb = min(cap, 64)
    return pl.pallas_call(
        functools.partial(_route_kernel, seq, cap, jb),
        grid=(b,),
        in_specs=[pl.BlockSpec((1, e, seq), lambda i: (i, 0, 0))],
        out_specs=[pl.BlockSpec((1, e, cap, 1), lambda i: (i, 0, 0, 0)),
                   pl.BlockSpec((1, e, cap, 1), lambda i: (i, 0, 0, 0))],
        out_shape=[jax.ShapeDtypeStruct((b, e, cap, 1), jnp.int32), jax.ShapeDtypeStruct((b, e, cap, 1), F32)],
        scratch_shapes=[pltpu.VMEM((e, seq), F32), pltpu.VMEM((e, seq), F32)],
        compiler_params=_params(("parallel",)), name="route",
    )(logits_t)


def _moe_up_kernel(cap, d_model, idx_ref, hg_ref, wg_ref, wu_ref, hid_ref, xg_scr, xs_scr):
    b, e, f = pl.program_id(0), pl.program_id(1), pl.program_id(2)

    @pl.when(f == 0)
    def _():
        base = (b * N_EXPERTS + e) * cap

        def copy(j, _):
            t = idx_ref[base + j]
            src = pl.multiple_of(t * BF16_SUBLANES, BF16_SUBLANES)
            dst = pl.multiple_of(j * BF16_SUBLANES, BF16_SUBLANES)
            xg_scr[pl.ds(dst, BF16_SUBLANES), :] = hg_ref[0, pl.ds(src, BF16_SUBLANES), :].astype(F32)
            return 0

        lax.fori_loop(0, cap, copy, 0, unroll=8)
        for c in range(d_model // LANES):
            xs_scr[:, c * LANES:(c + 1) * LANES] = xg_scr[pl.ds(c, cap, stride=BF16_SUBLANES), :].astype(BF16)

    xs = xs_scr[...]
    g = jnp.dot(xs, wg_ref[0], preferred_element_type=F32)
    u = jnp.dot(xs, wu_ref[0], preferred_element_type=F32)
    hid_ref[0, 0] = (g * _sigmoid(g) * u).astype(BF16)


def _moe_up(idx_flat, hg, wg, wu, *, cap, tf):
    b, rows, _ = hg.shape
    n_e, d_model, d_ff = wg.shape
    grid_spec = pltpu.PrefetchScalarGridSpec(
        num_scalar_prefetch=1, grid=(b, n_e, d_ff // tf),
        in_specs=[pl.BlockSpec((1, rows, LANES), lambda i, e, f, idx: (i, 0, 0), pipeline_mode=pl.Buffered(1)),
                  pl.BlockSpec((1, d_model, tf), lambda i, e, f, idx: (e, 0, f)),
                  pl.BlockSpec((1, d_model, tf), lambda i, e, f, idx: (e, 0, f))],
        out_specs=pl.BlockSpec((1, 1, cap, tf), lambda i, e, f, idx: (i, e, 0, f)),
        scratch_shapes=[pltpu.VMEM((cap * BF16_SUBLANES, LANES), F32), pltpu.VMEM((cap, d_model), BF16)])
    return pl.pallas_call(
        functools.partial(_moe_up_kernel, cap, d_model),
        grid_spec=grid_spec,
        out_shape=jax.ShapeDtypeStruct((b, n_e, cap, d_ff), BF16),
        compiler_params=_params(("arbitrary", "arbitrary", "arbitrary")), name="moe_up",
    )(idx_flat, hg, wg, wu)


def _moe_down_kernel(cap, half, idx_ref, hid_ref, gate_ref, wd_ref, acc_ref, og_scr):
    b, e = pl.program_id(0), pl.program_id(2)

    @pl.when(e == 0)
    def _():
        acc_ref[...] = jnp.zeros_like(acc_ref)

    o = jnp.dot(hid_ref[0, 0], wd_ref[0], preferred_element_type=F32) * gate_ref[0]
    for c in range(half // LANES):
        og_scr[pl.ds(c, cap, stride=SUBLANES), :] = o[:, c * LANES:(c + 1) * LANES]
    base = (b * N_EXPERTS + e) * cap

    def add(j, _):
        t = idx_ref[base + j]
        dst = pl.multiple_of(t * SUBLANES, SUBLANES)
        src = pl.multiple_of(j * SUBLANES, SUBLANES)
        acc_ref[0, 0, pl.ds(dst, SUBLANES), :] = acc_ref[0, 0, pl.ds(dst, SUBLANES), :] + og_scr[pl.ds(src, SUBLANES), :]
        return 0

    lax.fori_loop(0, cap, add, 0, unroll=8)


def _moe_down(idx_flat, hid, gate, wd, *, seq):
    b, n_e, cap, d_ff = hid.shape
    d_model = wd.shape[2]
    half = d_model // 2
    grid_spec = pltpu.PrefetchScalarGridSpec(
        num_scalar_prefetch=1, grid=(b, 2, n_e),
        in_specs=[pl.BlockSpec((1, 1, cap, d_ff), lambda i, p, e, idx: (i, e, 0, 0)),
                  pl.BlockSpec((1, cap, 1), lambda i, p, e, idx: (i * n_e + e, 0, 0)),
                  pl.BlockSpec((1, d_ff, half), lambda i, p, e, idx: (e, 0, p))],
        out_specs=pl.BlockSpec((1, 1, seq * SUBLANES, LANES), lambda i, p, e, idx: (i, p, 0, 0)),
        scratch_shapes=[pltpu.VMEM((cap * SUBLANES, LANES), F32)])
    return pl.pallas_call(
        functools.partial(_moe_down_kernel, cap, half),
        grid_spec=grid_spec,
        out_shape=jax.ShapeDtypeStruct((b, 2, seq * SUBLANES, LANES), F32),
        compiler_params=_params(("arbitrary", "arbitrary", "arbitrary")), name="moe_down",
    )(idx_flat, hid, gate, wd)


def _final_kernel(tm, x_ref, moe_ref, g2_ref, g_ref, y_ref):
    x = _moe_residual(x_ref[...], moe_ref, g2_ref[0], tm)
    y_ref[...] = _rms(x, g_ref[...])


def _final(x, moe, mod_prev, g, *, tm, tiles_per_mod):
    t, d = x.shape
    tiles_per_row = moe.shape[2] // (tm * SUBLANES)
    return pl.pallas_call(
        functools.partial(_final_kernel, tm),
        grid=(t // tm,),
        in_specs=[pl.BlockSpec((tm, d), lambda i: (i, 0)),
                  pl.BlockSpec((1, 2, tm * SUBLANES, LANES), lambda i: (i // tiles_per_row, 0, i % tiles_per_row, 0)),
                  pl.BlockSpec((1, 1, d), lambda i: (i // tiles_per_mod, 0, 5)),
                  pl.BlockSpec(g.shape, lambda i: (0, 0))],
        out_specs=pl.BlockSpec((tm, d), lambda i: (i, 0)),
        out_shape=jax.ShapeDtypeStruct((t, d), F32),
        compiler_params=_params(("parallel",)), name="final_norm",
    )(x, moe, mod_prev, g)


def _rope_swap_perm():
    q = QK_ROPE // 4
    return jnp.array(list(range(q, 2 * q)) + list(range(0, q)) + list(range(3 * q, 4 * q)) + list(range(2 * q, 3 * q)),
                     jnp.int32)


def _rope_table(n_tokens):
    rows = n_tokens // GRID_W
    row = jnp.repeat(jnp.arange(rows, dtype=F32), GRID_W)
    col = jnp.tile(jnp.arange(GRID_W, dtype=F32), rows)
    n_pairs = QK_ROPE // 4
    inv = ROPE_BASE ** (-jnp.arange(n_pairs, dtype=F32) / n_pairs)
    ang_r = row[:, None] * inv
    ang_c = col[:, None] * inv
    cr, sr, cc, sn = jnp.cos(ang_r), jnp.sin(ang_r), jnp.cos(ang_c), jnp.sin(ang_c)
    return jnp.concatenate([cr, cr, cc, cc, -sr, sr, -sn, sn], axis=1)


def _layer_weights(l, w):
    d = w["w_in"].shape[1]
    q_lora = w["q_norm_g"].shape[1]
    kv_lora = w["kv_norm_g"].shape[1]
    perm = _rope_swap_perm()
    w_in = w["w_in"][l]
    c1 = q_lora + kv_lora
    kr_w = w_in[:, c1:c1 + QK_ROPE]
    w_in_p = jnp.concatenate([w_in[:, :c1 + QK_ROPE], kr_w[:, perm], w_in[:, c1 + QK_ROPE:]], axis=1).astype(BF16)
    w_uq = w["w_uq"][l]
    rope_w = w_uq[:, :, QK_NOPE:]
    w_uq_p = jnp.concatenate([w_uq, rope_w[:, :, perm]], axis=2).reshape(q_lora, N_HEADS * HEAD_SLAB).astype(BF16)
    w_ukv = w["w_ukv"][l]
    w_uk_p = w_ukv[:, :, :QK_NOPE].reshape(kv_lora, -1).astype(BF16)
    w_vt_p = w_ukv[:, :, QK_NOPE:].reshape(kv_lora, -1).T.astype(BF16)
    return dict(
        w_in=w_in_p, q_norm_g=w["q_norm_g"][l][None], w_uq=w_uq_p, kv_norm_g=w["kv_norm_g"][l][None], w_uk=w_uk_p, w_vt=w_vt_p,
        norm1_g=w["norm1_g"][l][None], norm2_g=w["norm2_g"][l][None],
        sc_conv_w=w["sc_conv_w"][l], cf_conv_w=w["cf_conv_w"][l], cf_conv_b=w["cf_conv_b"][l][None],
        cf_ln_g=w["cf_ln_g"][l][None], cf_ln_b=w["cf_ln_b"][l][None],
        w_o=w["w_o"][l].astype(BF16), router_wt=w["router_w"][l].T,
        wg=w["exp_w_gate"][l].astype(BF16), wu=w["exp_w_up"][l].astype(BF16), wd=w["exp_w_down"][l].astype(BF16))


def _run_stream(x0, mods, lws, final_g, *, batch, seq, tab, cache_k, cache_v, merge_moe_rows, tm, tq, tk, n_hd, tf):
    t, d = x0.shape
    n_mod = mods[0].shape[0]
    tiles_per_mod = t // n_mod // tm
    tiles_per_seq = seq // tm
    cap = max(1, EC_CAPACITY * seq // N_EXPERTS)
    moe_b, moe_seq = (1, t) if merge_moe_rows else (batch, seq)
    moe_cap = cap * (batch // moe_b)
    x, moe = x0, None
    ckvs, krs = [], []
    for l, lw in enumerate(lws):
        x, q, k, v, cv, ckvn, kr = _inproj(x, moe, mods[l - 1] if l else None, mods[l], lw["norm1_g"], tab, lw,
                                           tm=tm, tiles_per_mod=tiles_per_mod, tiles_per_seq=tiles_per_seq)
        ckvs.append(ckvn)
        krs.append(kr)
        kc = None if cache_k is None else cache_k[l]
        vc = None if cache_v is None else cache_v[l]
        o_att = _attention(q, k, v, kc, vc, batch=batch, seq=seq, tq=tq, tk=tk, n_hd=n_hd)
        x, hg, logits_t = _mixout(o_att, cv, x, mods[l], lw, tm=tm, tiles_per_mod=tiles_per_mod,
                                  tiles_per_seq=tiles_per_seq, route_batch=batch, route_len=seq)
        idx, gate = _route(logits_t, cap)
        if moe_b != batch:
            offs = (jnp.arange(batch, dtype=jnp.int32) * seq)[:, None, None, None]
            idx = jnp.transpose(idx + offs, (1, 0, 2, 3)).reshape(1, N_EXPERTS, moe_cap, 1)
            gate = jnp.transpose(gate, (1, 0, 2, 3)).reshape(1, N_EXPERTS, moe_cap, 1)
        idx_flat = idx.reshape(-1)
        gate = gate.reshape(moe_b * N_EXPERTS, moe_cap, 1)
        hid = _moe_up(idx_flat, hg.reshape(moe_b, moe_seq * BF16_SUBLANES, LANES), lw["wg"], lw["wu"], cap=moe_cap, tf=tf)
        moe = _moe_down(idx_flat, hid, gate, lw["wd"], seq=moe_seq)
    y = _final(x, moe, mods[-1], final_g, tm=tm, tiles_per_mod=tiles_per_mod)
    return y, ckvs, krs


def kernel(x_prompt, x_sample, cache_ckv, cache_krope, c, c_ctx, ada_w, ada_b, norm1_g, norm2_g, w_in, q_norm_g, w_uq,
           kv_norm_g, w_ukv, sc_conv_w, cf_conv_w, cf_conv_b, cf_ln_g, cf_ln_b, w_o, router_w, exp_w_gate, exp_w_up,
           exp_w_down, final_norm_g):
    batch, seq, d = x_prompt.shape
    dec_batch, dec_seq, _ = x_sample.shape
    depth = ada_w.shape[0]
    past = cache_ckv.shape[2]
    w = dict(norm1_g=norm1_g, norm2_g=norm2_g, w_in=w_in, q_norm_g=q_norm_g, w_uq=w_uq, kv_norm_g=kv_norm_g, w_ukv=w_ukv,
             sc_conv_w=sc_conv_w, cf_conv_w=cf_conv_w, cf_conv_b=cf_conv_b, cf_ln_g=cf_ln_g, cf_ln_b=cf_ln_b, w_o=w_o,
             router_w=router_w, exp_w_gate=exp_w_gate, exp_w_up=exp_w_up, exp_w_down=exp_w_down)
    lws = [_layer_weights(l, w) for l in range(depth)]

    n_rows = -(-(1 + dec_batch) // SUBLANES) * SUBLANES
    c_all = jnp.zeros((n_rows, d), F32).at[0].set(c_ctx).at[1:1 + dec_batch].set(c)
    mod = _ada_mod(c_all, ada_w, ada_b)
    mods_p = [mod[l, 0:1][:, None, :] for l in range(depth)]
    mods_s = [mod[l, 1:1 + dec_batch][:, None, :] for l in range(depth)]

    tm = 256
    final_g = final_norm_g[None]

    ident = jnp.concatenate([jnp.ones((tm, QK_ROPE), F32), jnp.zeros((tm, QK_ROPE), F32)], axis=1)
    y_p, ckvs, krs = _run_stream(x_prompt.reshape(batch * seq, d), mods_p, lws, final_g, batch=batch, seq=seq, tab=ident,
                                 cache_k=None, cache_v=None, merge_moe_rows=True, tm=tm, tq=min(seq, 256), tk=min(seq, 512), n_hd=2, tf=512)

    ckv_c = jnp.transpose(cache_ckv, (1, 0, 2, 3))
    kr_c = jnp.transpose(cache_krope, (1, 0, 2, 3))
    kr_c = jnp.pad(kr_c, ((0, 0), (0, 0), (0, 0), (0, LANES - kr_c.shape[-1])))
    ck, cvv = _cache_kv(ckv_c, kr_c, jnp.stack([lw["w_uk"] for lw in lws]), jnp.stack([lw["w_vt"] for lw in lws]))
    y_s, _, _ = _run_stream(x_sample.reshape(dec_batch * dec_seq, d), mods_s, lws, final_g, batch=dec_batch, seq=dec_seq,
                            tab=_rope_table(dec_seq), cache_k=ck, cache_v=cvv, merge_moe_rows=False, tm=tm, tq=256, tk=min(512, dec_seq // 4), n_hd=2, tf=512)

    new_ckv = jnp.stack([a.reshape(batch, seq, -1) for a in ckvs], axis=1)
    new_kr = jnp.stack([a.reshape(batch, seq, -1) for a in krs], axis=1)
    return (y_p.reshape(batch, seq, d), y_s.reshape(dec_batch, dec_seq, d), new_ckv, new_kr)
```

```python
import functools

import jax
import jax.numpy as jnp
from jax import lax
from jax.experimental import pallas as pl
from jax.experimental.pallas import tpu as pltpu

LANES = 128
SUBLANES = 8
BF16_SUBLANES = 16
VMEM_LIMIT_BYTES = 56 * 1024 * 1024

N_HEADS = 8
QK_NOPE = 128
QK_ROPE = 64
V_DIM = 128
HEAD_SLAB = 256
N_EXPERTS = 16
EC_CAPACITY = 2
GRID_W = 64
SC_WIDTH = 3
CF_WIDTH = 31
CF_PAD = (CF_WIDTH - 1) // 2
HALO = 16
CONV_ROWS = 64
ROPE_BASE = 10000.0
NORM_EPS = 1e-6
ATTN_SCALE = (QK_NOPE + QK_ROPE) ** -0.5
LOG2_E = 1.4426950408889634
Q_SCALE = ATTN_SCALE * LOG2_E

F32 = jnp.float32
BF16 = jnp.bfloat16


def _params(sem):
    return pltpu.CompilerParams(dimension_semantics=sem, vmem_limit_bytes=VMEM_LIMIT_BYTES)


def _sigmoid(x):
    return 1.0 / (1.0 + jnp.exp(-x))


def _rms(x, g):
    return x * lax.rsqrt(jnp.mean(x * x, axis=-1, keepdims=True) + NORM_EPS) * g


def _split3_dot(a, b, dims):
    a_hi = a.astype(BF16)
    a_lo = (a - a_hi.astype(F32)).astype(BF16)
    b_hi = b.astype(BF16)
    b_lo = (b - b_hi.astype(F32)).astype(BF16)
    dot = functools.partial(lax.dot_general, dimension_numbers=(dims, ((), ())), preferred_element_type=F32)
    return dot(a_hi, b_hi) + dot(a_lo, b_hi) + dot(a_hi, b_lo)


def _ada_kernel(c_ref, w_ref, b_ref, o_ref):
    cv = c_ref[...]
    s = cv * _sigmoid(cv)
    o_ref[0] = _split3_dot(s, w_ref[0], ((1,), (0,))) + b_ref[0]


def _ada_mod(c_all, ada_w, ada_b):
    depth, d, n6 = ada_w.shape
    rows = c_all.shape[0]
    tn = 1024
    return pl.pallas_call(
        _ada_kernel,
        grid=(depth, n6 // tn),
        in_specs=[pl.BlockSpec((rows, d), lambda l, j: (0, 0)),
                  pl.BlockSpec((1, d, tn), lambda l, j: (l, 0, j)),
                  pl.BlockSpec((1, 1, tn), lambda l, j: (l, 0, j))],
        out_specs=pl.BlockSpec((1, rows, tn), lambda l, j: (l, 0, j)),
        out_shape=jax.ShapeDtypeStruct((depth, rows, n6), F32),
        compiler_params=_params(("parallel", "parallel")),
        name="ada_mod",
    )(c_all, ada_w, ada_b.reshape(depth, 1, n6))


def _moe_residual(x, moe_ref, g2, tm):
    chunks = []
    for p in range(2):
        for c in range(SUBLANES):
            chunks.append(moe_ref[0, p, pl.ds(c, tm, stride=SUBLANES), :])
    return x + g2 * jnp.concatenate(chunks, axis=1)


def _rope_pair(v, tab):
    prod = v * tab
    r = prod + pltpu.roll(prod, QK_ROPE, 1)
    lane = lax.broadcasted_iota(jnp.int32, r.shape, 1)
    return jnp.where(lane < QK_ROPE, r, 0.0)


def _inproj_kernel(has_moe, tm, d_model, q_lora, kv_lora, *refs):
    if has_moe:
        (x_ref, moe_ref, g2_ref, shsc_ref, n1_ref, tab_ref, win_ref, qg_ref, wuq_ref, kvg_ref, wuk_ref, wvt_ref,
         xo_ref, q_ref, k_ref, vt_ref, cv_ref, ckv_ref, kr_ref) = refs
        x = _moe_residual(x_ref[...], moe_ref, g2_ref[0], tm)
        xo_ref[...] = x
    else:
        (x_ref, shsc_ref, n1_ref, tab_ref, win_ref, qg_ref, wuq_ref, kvg_ref, wuk_ref, wvt_ref,
         q_ref, k_ref, vt_ref, cv_ref, ckv_ref, kr_ref) = refs
        x = x_ref[...]
    sh1 = shsc_ref[0, :, :d_model]
    sc1 = shsc_ref[0, :, d_model:]
    h = _rms(x, n1_ref[...]) * (1.0 + sc1) + sh1
    u = jnp.dot(h.astype(BF16), win_ref[...], preferred_element_type=F32)
    tab = tab_ref[...]

    cqn = _rms(u[:, :q_lora], qg_ref[...])
    qf = jnp.dot(cqn.astype(BF16), wuq_ref[...], preferred_element_type=F32)
    for hd in range(N_HEADS):
        o = hd * HEAD_SLAB
        q_ref[:, o:o + QK_NOPE] = (qf[:, o:o + QK_NOPE] * Q_SCALE).astype(BF16)
        q_ref[:, o + QK_NOPE:o + HEAD_SLAB] = (_rope_pair(qf[:, o + QK_NOPE:o + HEAD_SLAB], tab) * Q_SCALE).astype(BF16)

    c0 = q_lora
    ckvn = _rms(u[:, c0:c0 + kv_lora], kvg_ref[...])
    ckv_ref[...] = ckvn
    ckvb = ckvn.astype(BF16)
    knf = jnp.dot(ckvb, wuk_ref[...], preferred_element_type=F32)
    vt_ref[0] = lax.dot_general(wvt_ref[...], ckvb, (((1,), (1,)), ((), ())), preferred_element_type=F32).astype(BF16)
    c1 = c0 + kv_lora
    krv = u[:, c1:c1 + LANES]
    kr_ref[...] = krv[:, :QK_ROPE]
    kr2 = _rope_pair(krv, tab).astype(BF16)
    for hd in range(N_HEADS):
        o = hd * HEAD_SLAB
        k_ref[:, o:o + QK_NOPE] = knf[:, hd * QK_NOPE:(hd + 1) * QK_NOPE].astype(BF16)
        k_ref[:, o + QK_NOPE:o + HEAD_SLAB] = kr2
    cv_ref[...] = u[:, c1 + LANES:]


def _inproj(x, moe, mod_prev, mod, n1, tab, lw, *, tm, tiles_per_mod, tiles_per_seq):
    t, d = x.shape
    nt = t // tm
    p_in = lw["w_in"].shape[1]
    q_lora = lw["q_norm_g"].shape[1]
    kv_lora = lw["kv_norm_g"].shape[1]
    n_cv = p_in - q_lora - kv_lora - LANES
    has_moe = moe is not None
    tiles_per_row = None if moe is None else moe.shape[2] // (tm * SUBLANES)

    full = lambda a: pl.BlockSpec(a.shape, lambda i: (0,) * a.ndim)
    row = lambda w: pl.BlockSpec((tm, w), lambda i: (i, 0))
    in_specs = [row(d)]
    args = [x]
    if has_moe:
        in_specs += [pl.BlockSpec((1, 2, tm * SUBLANES, LANES), lambda i: (i // tiles_per_row, 0, i % tiles_per_row, 0)),
                     pl.BlockSpec((1, 1, d), lambda i: (i // tiles_per_mod, 0, 5))]
        args += [moe, mod_prev]
    in_specs += [pl.BlockSpec((1, 1, 2 * d), lambda i: (i // tiles_per_mod, 0, 0)),
                 full(n1),
                 pl.BlockSpec((tm, LANES), lambda i: (i % tiles_per_seq, 0)),
                 full(lw["w_in"]), full(lw["q_norm_g"]), full(lw["w_uq"]), full(lw["kv_norm_g"]), full(lw["w_uk"]), full(lw["w_vt"])]
    args += [mod, n1, tab, lw["w_in"], lw["q_norm_g"], lw["w_uq"], lw["kv_norm_g"], lw["w_uk"], lw["w_vt"]]

    out_specs, out_shape = [], []
    if has_moe:
        out_specs.append(row(d))
        out_shape.append(jax.ShapeDtypeStruct((t, d), F32))
    seq = tiles_per_seq * tm
    for w, dt in ((N_HEADS * HEAD_SLAB, BF16), (N_HEADS * HEAD_SLAB, BF16), (None, BF16),
                  (n_cv, F32), (kv_lora, F32), (QK_ROPE, F32)):
        if w is None:
            out_specs.append(pl.BlockSpec((1, N_HEADS * V_DIM, tm), lambda i: (i // tiles_per_seq, 0, i % tiles_per_seq)))
            out_shape.append(jax.ShapeDtypeStruct((t // seq, N_HEADS * V_DIM, seq), dt))
        else:
            out_specs.append(row(w))
            out_shape.append(jax.ShapeDtypeStruct((t, w), dt))

    outs = pl.pallas_call(
        functools.partial(_inproj_kernel, has_moe, tm, d, q_lora, kv_lora),
        grid=(nt,), in_specs=in_specs, out_specs=out_specs, out_shape=out_shape,
        compiler_params=_params(("parallel",)), name="inproj",
    )(*args)
    if not has_moe:
        outs = [x] + list(outs)
    return outs


def _cachekv_kernel(ckv_ref, kr_ref, wuk_ref, wvt_ref, k_ref, vt_ref):
    ckvb = ckv_ref[0, 0].astype(BF16)
    knf = jnp.dot(ckvb, wuk_ref[0], preferred_element_type=F32)
    kr2 = kr_ref[0, 0].astype(BF16)
    for hd in range(N_HEADS):
        o = hd * HEAD_SLAB
        k_ref[0, 0, :, o:o + QK_NOPE] = knf[:, hd * QK_NOPE:(hd + 1) * QK_NOPE].astype(BF16)
        k_ref[0, 0, :, o + QK_NOPE:o + HEAD_SLAB] = kr2
    vt_ref[0, 0] = lax.dot_general(wvt_ref[0], ckvb, (((1,), (1,)), ((), ())), preferred_element_type=F32).astype(BF16)


def _cache_kv(ckv_c, kr_c, w_uk_all, w_vt_all):
    depth, b, past, kv_lora = ckv_c.shape
    nk = N_HEADS * HEAD_SLAB
    nv = N_HEADS * V_DIM
    return pl.pallas_call(
        _cachekv_kernel,
        grid=(depth, b),
        in_specs=[pl.BlockSpec((1, 1, past, kv_lora), lambda l, i: (l, i, 0, 0)),
                  pl.BlockSpec((1, 1, past, LANES), lambda l, i: (l, i, 0, 0)),
                  pl.BlockSpec((1, kv_lora, N_HEADS * QK_NOPE), lambda l, i: (l, 0, 0)),
                  pl.BlockSpec((1, nv, kv_lora), lambda l, i: (l, 0, 0))],
        out_specs=[pl.BlockSpec((1, 1, past, nk), lambda l, i: (l, i, 0, 0)),
                   pl.BlockSpec((1, 1, nv, past), lambda l, i: (l, i, 0, 0))],
        out_shape=[jax.ShapeDtypeStruct((depth, b, past, nk), BF16), jax.ShapeDtypeStruct((depth, b, nv, past), BF16)],
        compiler_params=_params(("parallel", "parallel")), name="cache_kv",
    )(ckv_c, kr_c, w_uk_all, w_vt_all)


def _attn_kernel(has_ctx, n_hd, lk, tk, *refs):
    if has_ctx:
        q_ref, k_ref, vt_ref, kc_ref, vct_ref, o_ref, s_scr, sc_scr = refs
    else:
        q_ref, k_ref, vt_ref, o_ref, s_scr = refs
    tq = q_ref.shape[0]
    n = lk // tk
    qs = [q_ref[:, g * HEAD_SLAB:(g + 1) * HEAD_SLAB] for g in range(n_hd)]

    def scores(g, kc, s_ref):
        s = lax.dot_general(kc, qs[g], (((1,), (1,)), ((), ())), preferred_element_type=F32)
        s_ref[...] = s
        return jnp.max(s, axis=0, keepdims=True)

    def k_chunk(g, c):
        return k_ref[c * tk:(c + 1) * tk, g * HEAD_SLAB:(g + 1) * HEAD_SLAB]

    def vt_chunk(g, c):
        return vt_ref[0, g * V_DIM:(g + 1) * V_DIM, c * tk:(c + 1) * tk]

    def accumulate(state, s_ref, smax, vt):
        m, l, acc = state
        m_new = jnp.maximum(m, smax)
        alpha = jnp.exp2(m - m_new)
        p = jnp.exp2(s_ref[...] - m_new)
        l = alpha * l + jnp.sum(p, axis=0, keepdims=True)
        acc = alpha * acc + jnp.dot(vt, p.astype(BF16), preferred_element_type=F32)
        return m_new, l, acc

    states = [(jnp.full((1, tq), -jnp.inf, F32), jnp.zeros((1, tq), F32), jnp.zeros((V_DIM, tq), F32))] * n_hd
    smax = [scores(g, k_chunk(g, 0), s_scr.at[g, 0]) for g in range(n_hd)]
    for c in range(n):
        for g in range(n_hd):
            smax_next = None
            if c + 1 < n:
                smax_next = scores(g, k_chunk(g, c + 1), s_scr.at[g, (c + 1) % 2])
            elif has_ctx:
                smax_next = scores(g, kc_ref[0, :, g * HEAD_SLAB:(g + 1) * HEAD_SLAB], sc_scr.at[g])
            states[g] = accumulate(states[g], s_scr.at[g, c % 2], smax[g], vt_chunk(g, c))
            smax[g] = smax_next
    for g in range(n_hd):
        state = states[g]
        if has_ctx:
            state = accumulate(state, sc_scr.at[g], smax[g], vct_ref[0, g * V_DIM:(g + 1) * V_DIM, :])
        _, l, acc = state
        o_ref[:, g * V_DIM:(g + 1) * V_DIM] = jnp.transpose(acc / l).astype(BF16)


def _attention(q, k, vt, kc, vct, *, batch, seq, tq, tk, n_hd):
    t = q.shape[0]
    nq = seq // tq
    has_ctx = kc is not None
    in_specs = [pl.BlockSpec((tq, n_hd * HEAD_SLAB), lambda b, h, i: (b * nq + i, h)),
                pl.BlockSpec((seq, n_hd * HEAD_SLAB), lambda b, h, i: (b, h)),
                pl.BlockSpec((1, n_hd * V_DIM, seq), lambda b, h, i: (b, h, 0))]
    args = [q, k, vt]
    scratch = [pltpu.VMEM((n_hd, 2, tk, tq), F32)]
    if has_ctx:
        past = kc.shape[1]
        in_specs += [pl.BlockSpec((1, past, n_hd * HEAD_SLAB), lambda b, h, i: (b, 0, h)),
                     pl.BlockSpec((1, n_hd * V_DIM, past), lambda b, h, i: (b, h, 0))]
        args += [kc, vct]
        scratch += [pltpu.VMEM((n_hd, past, tq), F32)]
    return pl.pallas_call(
        functools.partial(_attn_kernel, has_ctx, n_hd, seq, tk),
        grid=(batch, N_HEADS // n_hd, nq), in_specs=in_specs,
        out_specs=pl.BlockSpec((tq, n_hd * V_DIM), lambda b, h, i: (b * nq + i, h)),
        out_shape=jax.ShapeDtypeStruct((t, N_HEADS * V_DIM), BF16),
        scratch_shapes=scratch,
        compiler_params=_params(("parallel", "parallel", "arbitrary")), name="attention",
    )(*args)


def _mixout_kernel(tm, tiles_per_seq, d_sc, d_cf,
                   o_ref, cv_ref, cvp_ref, cvn_ref, x_ref, g1_ref, sh2_ref, sc2_ref,
                   scw_ref, cfw_ref, cfb_ref, lng_ref, lnb_ref, wo_ref, n2_ref, rwt_ref,
                   xo_ref, hg_ref, lg_ref, ext_sc, ext_cf, ph_scr, z_scr, hg_scr):
    i = pl.program_id(0)
    first = (i % tiles_per_seq) == 0
    last = (i % tiles_per_seq) == tiles_per_seq - 1
    o_scx, o_scb, o_scc, o_cfa, o_cfg = 0, d_sc, 2 * d_sc, 3 * d_sc, 3 * d_sc + d_cf

    def sc_in(ref):
        return ref[:, o_scc:o_scc + d_sc] * ref[:, o_scx:o_scx + d_sc]

    def cf_in(ref):
        return ref[:, o_cfa:o_cfa + d_cf] * _sigmoid(ref[:, o_cfg:o_cfg + d_cf])

    ext_sc[0:HALO, :] = jnp.where(first, 0.0, sc_in(cvp_ref))
    ext_sc[HALO:HALO + tm, :] = sc_in(cv_ref)
    ext_sc[HALO + tm:, :] = jnp.where(last, 0.0, sc_in(cvn_ref))
    ext_cf[0:HALO, :] = jnp.where(first, 0.0, cf_in(cvp_ref))
    ext_cf[HALO:HALO + tm, :] = cf_in(cv_ref)
    ext_cf[HALO + tm:, :] = jnp.where(last, 0.0, cf_in(cvn_ref))

    sc_pad = (SC_WIDTH - 1) // 2
    y_sc = jnp.zeros((tm, d_sc), F32)
    for kk in range(SC_WIDTH):
        y_sc = y_sc + scw_ref[kk:kk + 1, :] * ext_sc[pl.ds(HALO - sc_pad + kk, tm), :]
    y_sc = cv_ref[:, o_scb:o_scb + d_sc] * y_sc

    first_row = HALO - CF_PAD
    span = tm + 2 * HALO - SUBLANES
    for phase in range(1, SUBLANES):
        ph_scr[phase - 1, 0:span, :] = ext_cf[pl.ds(phase, span), :]
    for ct in range(d_cf // LANES):
        cs = slice(ct * LANES, (ct + 1) * LANES)
        for rb in range(tm // CONV_ROWS):
            acc = jnp.zeros((CONV_ROWS, LANES), F32)
            for kk in range(CF_WIDTH):
                phase = (first_row + kk) % SUBLANES
                base = first_row + kk - phase + rb * CONV_ROWS
                src = ext_cf if phase == 0 else ph_scr.at[phase - 1]
                acc = acc + cfw_ref[kk:kk + 1, cs] * src[base:base + CONV_ROWS, cs]
            z_scr[rb * CONV_ROWS:(rb + 1) * CONV_ROWS, cs] = acc
    z = z_scr[...] + cfb_ref[...]
    mu = jnp.mean(z, axis=-1, keepdims=True)
    zc = z - mu
    var = jnp.mean(zc * zc, axis=-1, keepdims=True)
    z = zc * lax.rsqrt(var + NORM_EPS) * lng_ref[...] + lnb_ref[...]
    z = z * _sigmoid(z)

    mixed = jnp.concatenate([o_ref[...], y_sc.astype(BF16), z.astype(BF16)], axis=1)
    y = jnp.dot(mixed, wo_ref[...], preferred_element_type=F32)
    x = x_ref[...] + g1_ref[0] * y
    xo_ref[...] = x
    h2 = _rms(x, n2_ref[...]) * (1.0 + sc2_ref[0]) + sh2_ref[0]

    lg_ref[0] = _split3_dot(rwt_ref[...], h2, ((1,), (1,)))

    d_model = h2.shape[1]
    for c in range(d_model // LANES):
        hg_scr[pl.ds(c, tm, stride=BF16_SUBLANES), :] = h2[:, c * LANES:(c + 1) * LANES]
    hg_ref[...] = hg_scr[...].astype(BF16)


def _mixout(o_att, cv, x, mod, lw, *, tm, tiles_per_mod, tiles_per_seq, route_batch, route_len):
    t, d = x.shape
    nt = t // tm
    n_cv = cv.shape[1]
    d_sc = lw["sc_conv_w"].shape[1]
    d_cf = lw["cf_conv_w"].shape[1]
    hb = tm // HALO
    n_halo = t // HALO
    tiles_per_route = route_len // tm
    full = lambda a: pl.BlockSpec(a.shape, lambda i: (0,) * a.ndim)
    row = lambda w: pl.BlockSpec((tm, w), lambda i: (i, 0))
    modspec = lambda blk: pl.BlockSpec((1, 1, d), lambda i: (i // tiles_per_mod, 0, blk))
    in_specs = [row(o_att.shape[1]), row(n_cv),
                pl.BlockSpec((HALO, n_cv), lambda i: (jnp.maximum(i * hb - 1, 0), 0)),
                pl.BlockSpec((HALO, n_cv), lambda i: (jnp.minimum((i + 1) * hb, n_halo - 1), 0)),
                row(d), modspec(2), modspec(3), modspec(4),
                full(lw["sc_conv_w"]), full(lw["cf_conv_w"]), full(lw["cf_conv_b"]), full(lw["cf_ln_g"]),
                full(lw["cf_ln_b"]), full(lw["w_o"]), full(lw["norm2_g"]), full(lw["router_wt"])]
    args = [o_att, cv, cv, cv, x, mod, mod, mod, lw["sc_conv_w"], lw["cf_conv_w"], lw["cf_conv_b"], lw["cf_ln_g"],
            lw["cf_ln_b"], lw["w_o"], lw["norm2_g"], lw["router_wt"]]
    out_specs = [row(d),
                 pl.BlockSpec((tm * BF16_SUBLANES, LANES), lambda i: (i, 0)),
                 pl.BlockSpec((1, N_EXPERTS, tm), lambda i: (i // tiles_per_route, 0, i % tiles_per_route))]
    out_shape = [jax.ShapeDtypeStruct((t, d), F32),
                 jax.ShapeDtypeStruct((t * BF16_SUBLANES, LANES), BF16),
                 jax.ShapeDtypeStruct((route_batch, N_EXPERTS, route_len), F32)]
    return pl.pallas_call(
        functools.partial(_mixout_kernel, tm, tiles_per_seq, d_sc, d_cf),
        grid=(nt,), in_specs=in_specs, out_specs=out_specs, out_shape=out_shape,
        scratch_shapes=[pltpu.VMEM((tm + 2 * HALO, d_sc), F32), pltpu.VMEM((tm + 2 * HALO, d_cf), F32),
                        pltpu.VMEM((SUBLANES - 1, tm + 2 * HALO, d_cf), F32), pltpu.VMEM((tm, d_cf), F32),
                        pltpu.VMEM((tm * BF16_SUBLANES, LANES), F32)],
        compiler_params=_params(("parallel",)), name="mixout",
    )(*args)


def _route_kernel(seq, cap, jb, lg_ref, idx_ref, gate_ref, slot_scr, aff_scr):
    lg = lg_ref[0]
    ex = jnp.exp(lg - jnp.max(lg, axis=0, keepdims=True))
    aff = ex / jnp.sum(ex, axis=0, keepdims=True)
    bits = pltpu.bitcast(aff, jnp.int32)
    tok = lax.broadcasted_iota(jnp.int32, aff.shape, 1)
    capf = jnp.float32(cap)

    def count(mask):
        return jnp.sum(jnp.where(mask, 1.0, 0.0), axis=1, keepdims=True)

    thr = jnp.zeros((N_EXPERTS, 1), jnp.int32)
    for bit in range(30, -1, -1):
        cand = thr | jnp.int32(1 << bit)
        thr = jnp.where(count(bits >= cand) >= capf, cand, thr)
    gt = bits > thr
    eq = bits == thr
    need = capf - count(gt)
    cut = jnp.zeros((N_EXPERTS, 1), jnp.int32)
    for bit in range(max(seq.bit_length(), 1) - 1, -1, -1):
        cand = cut | jnp.int32(1 << bit)
        cut = jnp.where(count(eq & (tok < cand)) <= need, cand, cut)
    sel = gt | (eq & (tok < cut))

    r_i = lax.broadcasted_iota(jnp.int32, (LANES, LANES), 0)
    c_i = lax.broadcasted_iota(jnp.int32, (LANES, LANES), 1)
    tri = jnp.where(r_i < c_i, 1.0, 0.0).astype(BF16)
    self = jnp.where(sel, 1.0, 0.0)
    carry = jnp.zeros((N_EXPERTS, 1), F32)
    slots = []
    for blk in range(seq // LANES):
        seg = self[:, blk * LANES:(blk + 1) * LANES]
        excl = jnp.dot(seg.astype(BF16), tri, preferred_element_type=F32)
        slots.append(jnp.where(seg > 0.0, excl + carry, -1.0))
        carry = carry + jnp.sum(seg, axis=1, keepdims=True)
    slot_scr[...] = jnp.concatenate(slots, axis=1)
    aff_scr[...] = aff

    n_chunks = seq // LANES
    jcol = lax.broadcasted_iota(jnp.int32, (jb, LANES), 0).astype(F32)
    lane = lax.broadcasted_iota(jnp.int32, (jb, LANES), 1).astype(F32)

    for e in range(N_EXPERTS):
        def per_block(n, _, e=e):
            j0 = pl.multiple_of(n * jb, jb)
            want = jcol + j0.astype(F32)

            def per_chunk(cb, carry):
                idx_acc, g_acc = carry
                off = pl.multiple_of(cb * LANES, LANES)
                hit = slot_scr[e:e + 1, pl.ds(off, LANES)] == want
                idx_acc = idx_acc + jnp.where(hit, lane + off.astype(F32), 0.0)
                g_acc = g_acc + jnp.where(hit, aff_scr[e:e + 1, pl.ds(off, LANES)], 0.0)
                return idx_acc, g_acc

            zero = jnp.zeros((jb, LANES), F32)
            idx_acc, g_acc = lax.fori_loop(0, n_chunks, per_chunk, (zero, zero), unroll=min(n_chunks, 4))
            idx_ref[0, e, pl.ds(j0, jb), :] = jnp.sum(idx_acc, axis=1, keepdims=True).astype(jnp.int32)
            gate_ref[0, e, pl.ds(j0, jb), :] = jnp.sum(g_acc, axis=1, keepdims=True)
            return 0

        lax.fori_loop(0, cap // jb, per_block, 0)


def _route(logits_t, cap):
    b, e, seq = logits_t.shape
    jb = min(cap, 64)
    return pl.pallas_call(
        functools.partial(_route_kernel, seq, cap, jb),
        grid=(b,),
        in_specs=[pl.BlockSpec((1, e, seq), lambda i: (i, 0, 0))],
        out_specs=[pl.BlockSpec((1, e, cap, 1), lambda i: (i, 0, 0, 0)),
                   pl.BlockSpec((1, e, cap, 1), lambda i: (i, 0, 0, 0))],
        out_shape=[jax.ShapeDtypeStruct((b, e, cap, 1), jnp.int32), jax.ShapeDtypeStruct((b, e, cap, 1), F32)],
        scratch_shapes=[pltpu.VMEM((e, seq), F32), pltpu.VMEM((e, seq), F32)],
        compiler_params=_params(("parallel",)), name="route",
    )(logits_t)


def _moe_up_kernel(cap, d_model, pitch, idx_ref, hg_ref, wg_ref, wu_ref, hid_ref, xg_scr, xs_scr):
    b, e = pl.program_id(0), pl.program_id(1)

    def gather(expert, slot, j):
        t = idx_ref[(b * N_EXPERTS + expert) * cap + j]
        src = pl.multiple_of(t * BF16_SUBLANES, BF16_SUBLANES)
        tile = hg_ref[0, pl.ds(src, BF16_SUBLANES), :].astype(F32)
        xg_scr[slot, pl.ds(j, SUBLANES, stride=pitch), :] = tile[:SUBLANES]
        xg_scr[slot, pl.ds(SUBLANES * pitch + j, SUBLANES, stride=pitch), :] = tile[SUBLANES:]

    @pl.when(e == 0)
    def _():
        def first(j, _):
            gather(0, 0, j)
            return 0
        lax.fori_loop(0, cap, first, 0, unroll=8)

    slot = e % 2
    for c in range(d_model // LANES):
        xs_scr[:, c * LANES:(c + 1) * LANES] = xg_scr[slot, c * pitch:c * pitch + cap, :].astype(BF16)
    xs = xs_scr[...]
    g = jnp.dot(xs, wg_ref[0], preferred_element_type=F32)
    u = jnp.dot(xs, wu_ref[0], preferred_element_type=F32)
    hid_ref[0, 0] = (g * _sigmoid(g) * u).astype(BF16)
    nxt = jnp.minimum(e + 1, N_EXPERTS - 1)
    for j in range(cap):
        gather(nxt, 1 - slot, j)


def _moe_up(idx_flat, hg, wg, wu, *, cap):
    b, rows, _ = hg.shape
    n_e, d_model, d_ff = wg.shape
    pitch = cap + SUBLANES
    grid_spec = pltpu.PrefetchScalarGridSpec(
        num_scalar_prefetch=1, grid=(b, n_e),
        in_specs=[pl.BlockSpec((1, rows, LANES), lambda i, e, idx: (i, 0, 0), pipeline_mode=pl.Buffered(1)),
                  pl.BlockSpec((1, d_model, d_ff), lambda i, e, idx: (e, 0, 0)),
                  pl.BlockSpec((1, d_model, d_ff), lambda i, e, idx: (e, 0, 0))],
        out_specs=pl.BlockSpec((1, 1, cap, d_ff), lambda i, e, idx: (i, e, 0, 0)),
        scratch_shapes=[pltpu.VMEM((2, d_model // LANES * pitch, LANES), F32), pltpu.VMEM((cap, d_model), BF16)])
    return pl.pallas_call(
        functools.partial(_moe_up_kernel, cap, d_model, pitch),
        grid_spec=grid_spec,
        out_shape=jax.ShapeDtypeStruct((b, n_e, cap, d_ff), BF16),
        compiler_params=_params(("arbitrary", "arbitrary")), name="moe_up",
    )(idx_flat, hg, wg, wu)


def _moe_down_kernel(cap, half, pitch, group, idx_ref, hid_ref, gate_ref, wd_ref, acc_ref, og_scr):
    b, e = pl.program_id(0), pl.program_id(2)

    def project(slot):
        o = jnp.dot(hid_ref[0, 0], wd_ref[0], preferred_element_type=F32) * gate_ref[0]
        for c in range(half // LANES):
            og_scr[slot, c * pitch:c * pitch + cap, :] = o[:, c * LANES:(c + 1) * LANES]

    def scatter(expert, slot, j0):
        base = (b * N_EXPERTS + expert) * cap
        dsts = [pl.multiple_of(idx_ref[base + j0 + k] * SUBLANES, SUBLANES) for k in range(group)]
        vals = [acc_ref[0, 0, pl.ds(dsts[k], SUBLANES), :] + og_scr[slot, pl.ds(j0 + k, SUBLANES, stride=pitch), :]
                for k in range(group)]
        for k in range(group):
            acc_ref[0, 0, pl.ds(dsts[k], SUBLANES), :] = vals[k]

    slot = e % 2

    @pl.when(e == 0)
    def _():
        acc_ref[...] = jnp.zeros_like(acc_ref)
        project(slot)

    @pl.when(e > 0)
    def _():
        for jg in range(cap // group):
            scatter(e - 1, 1 - slot, jg * group)
        project(slot)

    @pl.when(e == N_EXPERTS - 1)
    def _():
        def flush(jg, _):
            scatter(e, slot, jg * group)
            return 0
        lax.fori_loop(0, cap // group, flush, 0)


def _moe_down(idx_flat, hid, gate, wd, *, seq):
    b, n_e, cap, d_ff = hid.shape
    d_model = wd.shape[2]
    half = d_model // 2
    pitch = cap + SUBLANES
    group = min(cap, 8)
    grid_spec = pltpu.PrefetchScalarGridSpec(
        num_scalar_prefetch=1, grid=(b, 2, n_e),
        in_specs=[pl.BlockSpec((1, 1, cap, d_ff), lambda i, p, e, idx: (i, e, 0, 0)),
                  pl.BlockSpec((1, cap, 1), lambda i, p, e, idx: (i * n_e + e, 0, 0)),
                  pl.BlockSpec((1, d_ff, half), lambda i, p, e, idx: (e, 0, p))],
        out_specs=pl.BlockSpec((1, 1, seq * SUBLANES, LANES), lambda i, p, e, idx: (i, p, 0, 0)),
        scratch_shapes=[pltpu.VMEM((2, half // LANES * pitch, LANES), F32)])
    return pl.pallas_call(
        functools.partial(_moe_down_kernel, cap, half, pitch, group),
        grid_spec=grid_spec,
        out_shape=jax.ShapeDtypeStruct((b, 2, seq * SUBLANES, LANES), F32),
        compiler_params=_params(("arbitrary", "arbitrary", "arbitrary")), name="moe_down",
    )(idx_flat, hid, gate, wd)


def _final_kernel(tm, x_ref, moe_ref, g2_ref, g_ref, y_ref):
    x = _moe_residual(x_ref[...], moe_ref, g2_ref[0], tm)
    y_ref[...] = _rms(x, g_ref[...])


def _final(x, moe, mod_prev, g, *, tm, tiles_per_mod):
    t, d = x.shape
    tiles_per_row = moe.shape[2] // (tm * SUBLANES)
    return pl.pallas_call(
        functools.partial(_final_kernel, tm),
        grid=(t // tm,),
        in_specs=[pl.BlockSpec((tm, d), lambda i: (i, 0)),
                  pl.BlockSpec((1, 2, tm * SUBLANES, LANES), lambda i: (i // tiles_per_row, 0, i % tiles_per_row, 0)),
                  pl.BlockSpec((1, 1, d), lambda i: (i // tiles_per_mod, 0, 5)),
                  pl.BlockSpec(g.shape, lambda i: (0, 0))],
        out_specs=pl.BlockSpec((tm, d), lambda i: (i, 0)),
        out_shape=jax.ShapeDtypeStruct((t, d), F32),
        compiler_params=_params(("parallel",)), name="final_norm",
    )(x, moe, mod_prev, g)


def _rope_swap_perm():
    q = QK_ROPE // 4
    return jnp.array(list(range(q, 2 * q)) + list(range(0, q)) + list(range(3 * q, 4 * q)) + list(range(2 * q, 3 * q)),
                     jnp.int32)


def _rope_table(n_tokens):
    rows = n_tokens // GRID_W
    row = jnp.repeat(jnp.arange(rows, dtype=F32), GRID_W)
    col = jnp.tile(jnp.arange(GRID_W, dtype=F32), rows)
    n_pairs = QK_ROPE // 4
    inv = ROPE_BASE ** (-jnp.arange(n_pairs, dtype=F32) / n_pairs)
    ang_r = row[:, None] * inv
    ang_c = col[:, None] * inv
    cr, sr, cc, sn = jnp.cos(ang_r), jnp.sin(ang_r), jnp.cos(ang_c), jnp.sin(ang_c)
    return jnp.concatenate([cr, cr, cc, cc, -sr, sr, -sn, sn], axis=1)


def _layer_weights(l, w):
    d = w["w_in"].shape[1]
    q_lora = w["q_norm_g"].shape[1]
    kv_lora = w["kv_norm_g"].shape[1]
    perm = _rope_swap_perm()
    w_in = w["w_in"][l]
    c1 = q_lora + kv_lora
    kr_w = w_in[:, c1:c1 + QK_ROPE]
    w_in_p = jnp.concatenate([w_in[:, :c1 + QK_ROPE], kr_w[:, perm], w_in[:, c1 + QK_ROPE:]], axis=1).astype(BF16)
    w_uq = w["w_uq"][l]
    rope_w = w_uq[:, :, QK_NOPE:]
    w_uq_p = jnp.concatenate([w_uq, rope_w[:, :, perm]], axis=2).reshape(q_lora, N_HEADS * HEAD_SLAB).astype(BF16)
    w_ukv = w["w_ukv"][l]
    w_uk_p = w_ukv[:, :, :QK_NOPE].reshape(kv_lora, -1).astype(BF16)
    w_vt_p = w_ukv[:, :, QK_NOPE:].reshape(kv_lora, -1).T.astype(BF16)
    return dict(
        w_in=w_in_p, q_norm_g=w["q_norm_g"][l][None], w_uq=w_uq_p, kv_norm_g=w["kv_norm_g"][l][None], w_uk=w_uk_p, w_vt=w_vt_p,
        norm1_g=w["norm1_g"][l][None], norm2_g=w["norm2_g"][l][None],
        sc_conv_w=w["sc_conv_w"][l], cf_conv_w=w["cf_conv_w"][l], cf_conv_b=w["cf_conv_b"][l][None],
        cf_ln_g=w["cf_ln_g"][l][None], cf_ln_b=w["cf_ln_b"][l][None],
        w_o=w["w_o"][l].astype(BF16), router_wt=w["router_w"][l].T,
        wg=w["exp_w_gate"][l].astype(BF16), wu=w["exp_w_up"][l].astype(BF16), wd=w["exp_w_down"][l].astype(BF16))


def _run_stream(x0, mods, lws, final_g, *, batch, seq, tab, cache_k, cache_v, merge_moe_rows, tm, tq, tk, n_hd):
    t, d = x0.shape
    n_mod = mods[0].shape[0]
    tiles_per_mod = t // n_mod // tm
    tiles_per_seq = seq // tm
    cap = max(1, EC_CAPACITY * seq // N_EXPERTS)
    moe_b, moe_seq = (1, t) if merge_moe_rows else (batch, seq)
    moe_cap = cap * (batch // moe_b)
    x, moe = x0, None
    ckvs, krs = [], []
    for l, lw in enumerate(lws):
        x, q, k, v, cv, ckvn, kr = _inproj(x, moe, mods[l - 1] if l else None, mods[l], lw["norm1_g"], tab, lw,
                                           tm=tm, tiles_per_mod=tiles_per_mod, tiles_per_seq=tiles_per_seq)
        ckvs.append(ckvn)
        krs.append(kr)
        kc = None if cache_k is None else cache_k[l]
        vc = None if cache_v is None else cache_v[l]
        o_att = _attention(q, k, v, kc, vc, batch=batch, seq=seq, tq=tq, tk=tk, n_hd=n_hd)
        x, hg, logits_t = _mixout(o_att, cv, x, mods[l], lw, tm=tm, tiles_per_mod=tiles_per_mod,
                                  tiles_per_seq=tiles_per_seq, route_batch=batch, route_len=seq)
        idx, gate = _route(logits_t, cap)
        if moe_b != batch:
            offs = (jnp.arange(batch, dtype=jnp.int32) * seq)[:, None, None, None]
            idx = jnp.transpose(idx + offs, (1, 0, 2, 3)).reshape(1, N_EXPERTS, moe_cap, 1)
            gate = jnp.transpose(gate, (1, 0, 2, 3)).reshape(1, N_EXPERTS, moe_cap, 1)
        idx_flat = idx.reshape(-1)
        gate = gate.reshape(moe_b * N_EXPERTS, moe_cap, 1)
        hid = _moe_up(idx_flat, hg.reshape(moe_b, moe_seq * BF16_SUBLANES, LANES), lw["wg"], lw["wu"], cap=moe_cap)
        moe = _moe_down(idx_flat, hid, gate, lw["wd"], seq=moe_seq)
    y = _final(x, moe, mods[-1], final_g, tm=tm, tiles_per_mod=tiles_per_mod)
    return y, ckvs, krs


def kernel(x_prompt, x_sample, cache_ckv, cache_krope, c, c_ctx, ada_w, ada_b, norm1_g, norm2_g, w_in, q_norm_g, w_uq,
           kv_norm_g, w_ukv, sc_conv_w, cf_conv_w, cf_conv_b, cf_ln_g, cf_ln_b, w_o, router_w, exp_w_gate, exp_w_up,
           exp_w_down, final_norm_g):
    batch, seq, d = x_prompt.shape
    dec_batch, dec_seq, _ = x_sample.shape
    depth = ada_w.shape[0]
    past = cache_ckv.shape[2]
    w = dict(norm1_g=norm1_g, norm2_g=norm2_g, w_in=w_in, q_norm_g=q_norm_g, w_uq=w_uq, kv_norm_g=kv_norm_g, w_ukv=w_ukv,
             sc_conv_w=sc_conv_w, cf_conv_w=cf_conv_w, cf_conv_b=cf_conv_b, cf_ln_g=cf_ln_g, cf_ln_b=cf_ln_b, w_o=w_o,
             router_w=router_w, exp_w_gate=exp_w_gate, exp_w_up=exp_w_up, exp_w_down=exp_w_down)
    lws = [_layer_weights(l, w) for l in range(depth)]

    n_rows = -(-(1 + dec_batch) // SUBLANES) * SUBLANES
    c_all = jnp.zeros((n_rows, d), F32).at[0].set(c_ctx).at[1:1 + dec_batch].set(c)
    mod = _ada_mod(c_all, ada_w, ada_b)
    mods_p = [mod[l, 0:1][:, None, :] for l in range(depth)]
    mods_s = [mod[l, 1:1 + dec_batch][:, None, :] for l in range(depth)]

    tm = 256
    final_g = final_norm_g[None]

    ident = jnp.concatenate([jnp.ones((tm, QK_ROPE), F32), jnp.zeros((tm, QK_ROPE), F32)], axis=1)
    y_p, ckvs, krs = _run_stream(x_prompt.reshape(batch * seq, d), mods_p, lws, final_g, batch=batch, seq=seq, tab=ident,
                                 cache_k=None, cache_v=None, merge_moe_rows=True, tm=tm, tq=min(seq, 256), tk=min(seq, 512), n_hd=2)

    ckv_c = jnp.transpose(cache_ckv, (1, 0, 2, 3))
    kr_c = jnp.transpose(cache_krope, (1, 0, 2, 3))
    kr_c = jnp.pad(kr_c, ((0, 0), (0, 0), (0, 0), (0, LANES - kr_c.shape[-1])))
    ck, cvv = _cache_kv(ckv_c, kr_c, jnp.stack([lw["w_uk"] for lw in lws]), jnp.stack([lw["w_vt"] for lw in lws]))
    y_s, _, _ = _run_stream(x_sample.reshape(dec_batch * dec_seq, d), mods_s, lws, final_g, batch=dec_batch, seq=dec_seq,
                            tab=_rope_table(dec_seq), cache_k=ck, cache_v=cvv, merge_moe_rows=False, tm=tm, tq=256, tk=min(512, dec_seq // 4), n_hd=2)

    new_ckv = jnp.stack([a.reshape(batch, seq, -1) for a in ckvs], axis=1)
    new_kr = jnp.stack([a.reshape(batch, seq, -1) for a in krs], axis=1)
    return (y_p.reshape(batch, seq, d), y_s.reshape(dec_batch, dec_seq, d), new_ckv, new_kr)
```

```python
import functools

import jax
import jax.numpy as jnp
from jax import lax
from jax.experimental import pallas as pl
from jax.experimental.pallas import tpu as pltpu

LANES = 128
SUBLANES = 8
BF16_SUBLANES = 16
VMEM_LIMIT_BYTES = 56 * 1024 * 1024

N_HEADS = 8
QK_NOPE = 128
QK_ROPE = 64
V_DIM = 128
HEAD_SLAB = 256
N_EXPERTS = 16
EC_CAPACITY = 2
GRID_W = 64
SC_WIDTH = 3
CF_WIDTH = 31
CF_PAD = (CF_WIDTH - 1) // 2
HALO = 16
CONV_ROWS = 64
ROPE_BASE = 10000.0
NORM_EPS = 1e-6
ATTN_SCALE = (QK_NOPE + QK_ROPE) ** -0.5
LOG2_E = 1.4426950408889634
Q_SCALE = ATTN_SCALE * LOG2_E

F32 = jnp.float32
BF16 = jnp.bfloat16


def _params(sem):
    return pltpu.CompilerParams(dimension_semantics=sem, vmem_limit_bytes=VMEM_LIMIT_BYTES)


def _sigmoid(x):
    return 1.0 / (1.0 + jnp.exp(-x))


def _rms(x, g):
    return x * lax.rsqrt(jnp.mean(x * x, axis=-1, keepdims=True) + NORM_EPS) * g


def _split3_dot(a, b, dims):
    a_hi = a.astype(BF16)
    a_lo = (a - a_hi.astype(F32)).astype(BF16)
    b_hi = b.astype(BF16)
    b_lo = (b - b_hi.astype(F32)).astype(BF16)
    dot = functools.partial(lax.dot_general, dimension_numbers=(dims, ((), ())), preferred_element_type=F32)
    return dot(a_hi, b_hi) + dot(a_lo, b_hi) + dot(a_hi, b_lo)


def _ada_kernel(c_ref, w_ref, b_ref, o_ref):
    cv = c_ref[...]
    s = cv * _sigmoid(cv)
    o_ref[0] = _split3_dot(s, w_ref[0], ((1,), (0,))) + b_ref[0]


def _ada_mod(c_all, ada_w, ada_b):
    depth, d, n6 = ada_w.shape
    rows = c_all.shape[0]
    tn = 1024
    return pl.pallas_call(
        _ada_kernel,
        grid=(depth, n6 // tn),
        in_specs=[pl.BlockSpec((rows, d), lambda l, j: (0, 0)),
                  pl.BlockSpec((1, d, tn), lambda l, j: (l, 0, j)),
                  pl.BlockSpec((1, 1, tn), lambda l, j: (l, 0, j))],
        out_specs=pl.BlockSpec((1, rows, tn), lambda l, j: (l, 0, j)),
        out_shape=jax.ShapeDtypeStruct((depth, rows, n6), F32),
        compiler_params=_params(("parallel", "parallel")),
        name="ada_mod",
    )(c_all, ada_w, ada_b.reshape(depth, 1, n6))


def _moe_residual(x, moe_ref, g2, tm):
    chunks = []
    for p in range(2):
        for c in range(SUBLANES):
            chunks.append(moe_ref[0, p, pl.ds(c, tm, stride=SUBLANES), :])
    return x + g2 * jnp.concatenate(chunks, axis=1)


def _rope_pair(v, tab):
    prod = v * tab
    r = prod + pltpu.roll(prod, QK_ROPE, 1)
    lane = lax.broadcasted_iota(jnp.int32, r.shape, 1)
    return jnp.where(lane < QK_ROPE, r, 0.0)


def _inproj_kernel(has_moe, tm, d_model, q_lora, kv_lora, *refs):
    if has_moe:
        (x_ref, moe_ref, g2_ref, shsc_ref, n1_ref, tab_ref, win_ref, qg_ref, wuq_ref, kvg_ref, wuk_ref, wvt_ref,
         xo_ref, q_ref, k_ref, vt_ref, cv_ref, ckv_ref, kr_ref) = refs
        x = _moe_residual(x_ref[...], moe_ref, g2_ref[0], tm)
        xo_ref[...] = x
    else:
        (x_ref, shsc_ref, n1_ref, tab_ref, win_ref, qg_ref, wuq_ref, kvg_ref, wuk_ref, wvt_ref,
         q_ref, k_ref, vt_ref, cv_ref, ckv_ref, kr_ref) = refs
        x = x_ref[...]
    sh1 = shsc_ref[0, :, :d_model]
    sc1 = shsc_ref[0, :, d_model:]
    h = _rms(x, n1_ref[...]) * (1.0 + sc1) + sh1
    u = jnp.dot(h.astype(BF16), win_ref[...], preferred_element_type=F32)
    tab = tab_ref[...]

    cqn = _rms(u[:, :q_lora], qg_ref[...])
    qf = jnp.dot(cqn.astype(BF16), wuq_ref[...], preferred_element_type=F32)
    for hd in range(N_HEADS):
        o = hd * HEAD_SLAB
        q_ref[:, o:o + QK_NOPE] = (qf[:, o:o + QK_NOPE] * Q_SCALE).astype(BF16)
        q_ref[:, o + QK_NOPE:o + HEAD_SLAB] = (_rope_pair(qf[:, o + QK_NOPE:o + HEAD_SLAB], tab) * Q_SCALE).astype(BF16)

    c0 = q_lora
    ckvn = _rms(u[:, c0:c0 + kv_lora], kvg_ref[...])
    ckv_ref[...] = ckvn
    ckvb = ckvn.astype(BF16)
    knf = jnp.dot(ckvb, wuk_ref[...], preferred_element_type=F32)
    vt_ref[0] = lax.dot_general(wvt_ref[...], ckvb, (((1,), (1,)), ((), ())), preferred_element_type=F32).astype(BF16)
    c1 = c0 + kv_lora
    krv = u[:, c1:c1 + LANES]
    kr_ref[...] = krv[:, :QK_ROPE]
    kr2 = _rope_pair(krv, tab).astype(BF16)
    for hd in range(N_HEADS):
        o = hd * HEAD_SLAB
        k_ref[:, o:o + QK_NOPE] = knf[:, hd * QK_NOPE:(hd + 1) * QK_NOPE].astype(BF16)
        k_ref[:, o + QK_NOPE:o + HEAD_SLAB] = kr2
    cv_ref[...] = u[:, c1 + LANES:]


def _inproj(x, moe, mod_prev, mod, n1, tab, lw, *, tm, tiles_per_mod, tiles_per_seq):
    t, d = x.shape
    nt = t // tm
    p_in = lw["w_in"].shape[1]
    q_lora = lw["q_norm_g"].shape[1]
    kv_lora = lw["kv_norm_g"].shape[1]
    n_cv = p_in - q_lora - kv_lora - LANES
    has_moe = moe is not None
    tiles_per_row = None if moe is None else moe.shape[2] // (tm * SUBLANES)

    full = lambda a: pl.BlockSpec(a.shape, lambda i: (0,) * a.ndim)
    row = lambda w: pl.BlockSpec((tm, w), lambda i: (i, 0))
    in_specs = [row(d)]
    args = [x]
    if has_moe:
        in_specs += [pl.BlockSpec((1, 2, tm * SUBLANES, LANES), lambda i: (i // tiles_per_row, 0, i % tiles_per_row, 0)),
                     pl.BlockSpec((1, 1, d), lambda i: (i // tiles_per_mod, 0, 5))]
        args += [moe, mod_prev]
    in_specs += [pl.BlockSpec((1, 1, 2 * d), lambda i: (i // tiles_per_mod, 0, 0)),
                 full(n1),
                 pl.BlockSpec((tm, LANES), lambda i: (i % tiles_per_seq, 0)),
                 full(lw["w_in"]), full(lw["q_norm_g"]), full(lw["w_uq"]), full(lw["kv_norm_g"]), full(lw["w_uk"]), full(lw["w_vt"])]
    args += [mod, n1, tab, lw["w_in"], lw["q_norm_g"], lw["w_uq"], lw["kv_norm_g"], lw["w_uk"], lw["w_vt"]]

    out_specs, out_shape = [], []
    if has_moe:
        out_specs.append(row(d))
        out_shape.append(jax.ShapeDtypeStruct((t, d), F32))
    seq = tiles_per_seq * tm
    for w, dt in ((N_HEADS * HEAD_SLAB, BF16), (N_HEADS * HEAD_SLAB, BF16), (None, BF16),
                  (n_cv, F32), (kv_lora, F32), (QK_ROPE, F32)):
        if w is None:
            out_specs.append(pl.BlockSpec((1, N_HEADS * V_DIM, tm), lambda i: (i // tiles_per_seq, 0, i % tiles_per_seq)))
            out_shape.append(jax.ShapeDtypeStruct((t // seq, N_HEADS * V_DIM, seq), dt))
        else:
            out_specs.append(row(w))
            out_shape.append(jax.ShapeDtypeStruct((t, w), dt))

    outs = pl.pallas_call(
        functools.partial(_inproj_kernel, has_moe, tm, d, q_lora, kv_lora),
        grid=(nt,), in_specs=in_specs, out_specs=out_specs, out_shape=out_shape,
        compiler_params=_params(("parallel",)), name="inproj",
    )(*args)
    if not has_moe:
        outs = [x] + list(outs)
    return outs


def _cachekv_kernel(ckv_ref, kr_ref, wuk_ref, wvt_ref, k_ref, vt_ref):
    ckvb = ckv_ref[0, 0].astype(BF16)
    knf = jnp.dot(ckvb, wuk_ref[0], preferred_element_type=F32)
    kr2 = kr_ref[0, 0].astype(BF16)
    for hd in range(N_HEADS):
        o = hd * HEAD_SLAB
        k_ref[0, 0, :, o:o + QK_NOPE] = knf[:, hd * QK_NOPE:(hd + 1) * QK_NOPE].astype(BF16)
        k_ref[0, 0, :, o + QK_NOPE:o + HEAD_SLAB] = kr2
    vt_ref[0, 0] = lax.dot_general(wvt_ref[0], ckvb, (((1,), (1,)), ((), ())), preferred_element_type=F32).astype(BF16)


def _cache_kv(ckv_c, kr_c, w_uk_all, w_vt_all):
    depth, b, past, kv_lora = ckv_c.shape
    nk = N_HEADS * HEAD_SLAB
    nv = N_HEADS * V_DIM
    return pl.pallas_call(
        _cachekv_kernel,
        grid=(depth, b),
        in_specs=[pl.BlockSpec((1, 1, past, kv_lora), lambda l, i: (l, i, 0, 0)),
                  pl.BlockSpec((1, 1, past, LANES), lambda l, i: (l, i, 0, 0)),
                  pl.BlockSpec((1, kv_lora, N_HEADS * QK_NOPE), lambda l, i: (l, 0, 0)),
                  pl.BlockSpec((1, nv, kv_lora), lambda l, i: (l, 0, 0))],
        out_specs=[pl.BlockSpec((1, 1, past, nk), lambda l, i: (l, i, 0, 0)),
                   pl.BlockSpec((1, 1, nv, past), lambda l, i: (l, i, 0, 0))],
        out_shape=[jax.ShapeDtypeStruct((depth, b, past, nk), BF16), jax.ShapeDtypeStruct((depth, b, nv, past), BF16)],
        compiler_params=_params(("parallel", "parallel")), name="cache_kv",
    )(ckv_c, kr_c, w_uk_all, w_vt_all)


def _attn_kernel(has_ctx, n_hd, ahead, lk, tk, *refs):
    if has_ctx:
        q_ref, k_ref, vt_ref, kc_ref, vct_ref, o_ref, s_scr, sc_scr = refs
    else:
        q_ref, k_ref, vt_ref, o_ref, s_scr = refs
    tq = q_ref.shape[0]
    n = lk // tk
    qs = [q_ref[:, g * HEAD_SLAB:(g + 1) * HEAD_SLAB] for g in range(n_hd)]

    def scores(g, kc, s_ref):
        s = lax.dot_general(kc, qs[g], (((1,), (1,)), ((), ())), preferred_element_type=F32)
        s_ref[...] = s
        return jnp.max(s, axis=0, keepdims=True)

    def k_chunk(g, c):
        return k_ref[c * tk:(c + 1) * tk, g * HEAD_SLAB:(g + 1) * HEAD_SLAB]

    def vt_chunk(g, c):
        return vt_ref[0, g * V_DIM:(g + 1) * V_DIM, c * tk:(c + 1) * tk]

    def accumulate(state, s_ref, smax, vt):
        m, l, acc = state
        m_new = jnp.maximum(m, smax)
        alpha = jnp.exp2(m - m_new)
        p = jnp.exp2(s_ref[...] - m_new)
        l = alpha * l + jnp.sum(p, axis=0, keepdims=True)
        acc = alpha * acc + jnp.dot(vt, p.astype(BF16), preferred_element_type=F32)
        return m_new, l, acc

    total = n + (1 if has_ctx else 0)
    n_slots = s_scr.shape[1]

    def s_of(g, c):
        return s_scr.at[g, c % n_slots] if c < n else sc_scr.at[g]

    def issue(g, c):
        kc = k_chunk(g, c) if c < n else kc_ref[0, :, g * HEAD_SLAB:(g + 1) * HEAD_SLAB]
        return scores(g, kc, s_of(g, c))

    def vt_of(g, c):
        return vt_chunk(g, c) if c < n else vct_ref[0, g * V_DIM:(g + 1) * V_DIM, :]

    states = [(jnp.full((1, tq), -jnp.inf, F32), jnp.zeros((1, tq), F32), jnp.zeros((V_DIM, tq), F32))] * n_hd
    smax = {}
    for c in range(min(ahead, total)):
        for g in range(n_hd):
            smax[g, c] = issue(g, c)
    for c in range(total):
        for g in range(n_hd):
            if c + ahead < total:
                smax[g, c + ahead] = issue(g, c + ahead)
            states[g] = accumulate(states[g], s_of(g, c), smax.pop((g, c)), vt_of(g, c))
    for g in range(n_hd):
        _, l, acc = states[g]
        o_ref[:, g * V_DIM:(g + 1) * V_DIM] = jnp.transpose(acc / l).astype(BF16)


def _attention(q, k, vt, kc, vct, *, batch, seq, tq, tk, n_hd, ahead):
    t = q.shape[0]
    nq = seq // tq
    has_ctx = kc is not None
    in_specs = [pl.BlockSpec((tq, n_hd * HEAD_SLAB), lambda b, h, i: (b * nq + i, h)),
                pl.BlockSpec((seq, n_hd * HEAD_SLAB), lambda b, h, i: (b, h)),
                pl.BlockSpec((1, n_hd * V_DIM, seq), lambda b, h, i: (b, h, 0))]
    args = [q, k, vt]
    scratch = [pltpu.VMEM((n_hd, ahead + 1, tk, tq), F32)]
    if has_ctx:
        past = kc.shape[1]
        in_specs += [pl.BlockSpec((1, past, n_hd * HEAD_SLAB), lambda b, h, i: (b, 0, h)),
                     pl.BlockSpec((1, n_hd * V_DIM, past), lambda b, h, i: (b, h, 0))]
        args += [kc, vct]
        scratch += [pltpu.VMEM((n_hd, past, tq), F32)]
    return pl.pallas_call(
        functools.partial(_attn_kernel, has_ctx, n_hd, ahead, seq, tk),
        grid=(batch, N_HEADS // n_hd, nq), in_specs=in_specs,
        out_specs=pl.BlockSpec((tq, n_hd * V_DIM), lambda b, h, i: (b * nq + i, h)),
        out_shape=jax.ShapeDtypeStruct((t, N_HEADS * V_DIM), BF16),
        scratch_shapes=scratch,
        compiler_params=_params(("parallel", "parallel", "arbitrary")), name="attention",
    )(*args)


def _mixout_kernel(tm, tiles_per_seq, d_sc, d_cf,
                   o_ref, cv_ref, cvp_ref, cvn_ref, x_ref, g1_ref, sh2_ref, sc2_ref,
                   scw_ref, cfw_ref, cfb_ref, lng_ref, lnb_ref, wo_ref, n2_ref, rwt_ref,
                   xo_ref, hg_ref, lg_ref, ext_sc, ext_cf, ph_scr, z_scr, hg_scr):
    i = pl.program_id(0)
    first = (i % tiles_per_seq) == 0
    last = (i % tiles_per_seq) == tiles_per_seq - 1
    o_scx, o_scb, o_scc, o_cfa, o_cfg = 0, d_sc, 2 * d_sc, 3 * d_sc, 3 * d_sc + d_cf

    def sc_in(ref):
        return ref[:, o_scc:o_scc + d_sc] * ref[:, o_scx:o_scx + d_sc]

    def cf_in(ref):
        return ref[:, o_cfa:o_cfa + d_cf] * _sigmoid(ref[:, o_cfg:o_cfg + d_cf])

    ext_sc[0:HALO, :] = jnp.where(first, 0.0, sc_in(cvp_ref))
    ext_sc[HALO:HALO + tm, :] = sc_in(cv_ref)
    ext_sc[HALO + tm:, :] = jnp.where(last, 0.0, sc_in(cvn_ref))
    ext_cf[0:HALO, :] = jnp.where(first, 0.0, cf_in(cvp_ref))
    ext_cf[HALO:HALO + tm, :] = cf_in(cv_ref)
    ext_cf[HALO + tm:, :] = jnp.where(last, 0.0, cf_in(cvn_ref))

    sc_pad = (SC_WIDTH - 1) // 2
    y_sc = jnp.zeros((tm, d_sc), F32)
    for kk in range(SC_WIDTH):
        y_sc = y_sc + scw_ref[kk:kk + 1, :] * ext_sc[pl.ds(HALO - sc_pad + kk, tm), :]
    y_sc = cv_ref[:, o_scb:o_scb + d_sc] * y_sc

    first_row = HALO - CF_PAD
    span = tm + 2 * HALO - SUBLANES
    for phase in range(1, SUBLANES):
        ph_scr[phase - 1, 0:span, :] = ext_cf[pl.ds(phase, span), :]
    for ct in range(d_cf // LANES):
        cs = slice(ct * LANES, (ct + 1) * LANES)
        for rb in range(tm // CONV_ROWS):
            acc = jnp.zeros((CONV_ROWS, LANES), F32)
            for kk in range(CF_WIDTH):
                phase = (first_row + kk) % SUBLANES
                base = first_row + kk - phase + rb * CONV_ROWS
                src = ext_cf if phase == 0 else ph_scr.at[phase - 1]
                acc = acc + cfw_ref[kk:kk + 1, cs] * src[base:base + CONV_ROWS, cs]
            z_scr[rb * CONV_ROWS:(rb + 1) * CONV_ROWS, cs] = acc
    z = z_scr[...] + cfb_ref[...]
    mu = jnp.mean(z, axis=-1, keepdims=True)
    zc = z - mu
    var = jnp.mean(zc * zc, axis=-1, keepdims=True)
    z = zc * lax.rsqrt(var + NORM_EPS) * lng_ref[...] + lnb_ref[...]
    z = z * _sigmoid(z)

    mixed = jnp.concatenate([o_ref[...], y_sc.astype(BF16), z.astype(BF16)], axis=1)
    y = jnp.dot(mixed, wo_ref[...], preferred_element_type=F32)
    x = x_ref[...] + g1_ref[0] * y
    xo_ref[...] = x
    h2 = _rms(x, n2_ref[...]) * (1.0 + sc2_ref[0]) + sh2_ref[0]

    lg_ref[0] = _split3_dot(rwt_ref[...], h2, ((1,), (1,)))

    d_model = h2.shape[1]
    for c in range(d_model // LANES):
        hg_scr[pl.ds(c, tm, stride=BF16_SUBLANES), :] = h2[:, c * LANES:(c + 1) * LANES]
    hg_ref[...] = hg_scr[...].astype(BF16)


def _mixout(o_att, cv, x, mod, lw, *, tm, tiles_per_mod, tiles_per_seq, route_batch, route_len):
    t, d = x.shape
    nt = t // tm
    n_cv = cv.shape[1]
    d_sc = lw["sc_conv_w"].shape[1]
    d_cf = lw["cf_conv_w"].shape[1]
    hb = tm // HALO
    n_halo = t // HALO
    tiles_per_route = route_len // tm
    full = lambda a: pl.BlockSpec(a.shape, lambda i: (0,) * a.ndim)
    row = lambda w: pl.BlockSpec((tm, w), lambda i: (i, 0))
    modspec = lambda blk: pl.BlockSpec((1, 1, d), lambda i: (i // tiles_per_mod, 0, blk))
    in_specs = [row(o_att.shape[1]), row(n_cv),
                pl.BlockSpec((HALO, n_cv), lambda i: (jnp.maximum(i * hb - 1, 0), 0)),
                pl.BlockSpec((HALO, n_cv), lambda i: (jnp.minimum((i + 1) * hb, n_halo - 1), 0)),
                row(d), modspec(2), modspec(3), modspec(4),
                full(lw["sc_conv_w"]), full(lw["cf_conv_w"]), full(lw["cf_conv_b"]), full(lw["cf_ln_g"]),
                full(lw["cf_ln_b"]), full(lw["w_o"]), full(lw["norm2_g"]), full(lw["router_wt"])]
    args = [o_att, cv, cv, cv, x, mod, mod, mod, lw["sc_conv_w"], lw["cf_conv_w"], lw["cf_conv_b"], lw["cf_ln_g"],
            lw["cf_ln_b"], lw["w_o"], lw["norm2_g"], lw["router_wt"]]
    out_specs = [row(d),
                 pl.BlockSpec((tm * BF16_SUBLANES, LANES), lambda i: (i, 0)),
                 pl.BlockSpec((1, N_EXPERTS, tm), lambda i: (i // tiles_per_route, 0, i % tiles_per_route))]
    out_shape = [jax.ShapeDtypeStruct((t, d), F32),
                 jax.ShapeDtypeStruct((t * BF16_SUBLANES, LANES), BF16),
                 jax.ShapeDtypeStruct((route_batch, N_EXPERTS, route_len), F32)]
    return pl.pallas_call(
        functools.partial(_mixout_kernel, tm, tiles_per_seq, d_sc, d_cf),
        grid=(nt,), in_specs=in_specs, out_specs=out_specs, out_shape=out_shape,
        scratch_shapes=[pltpu.VMEM((tm + 2 * HALO, d_sc), F32), pltpu.VMEM((tm + 2 * HALO, d_cf), F32),
                        pltpu.VMEM((SUBLANES - 1, tm + 2 * HALO, d_cf), F32), pltpu.VMEM((tm, d_cf), F32),
                        pltpu.VMEM((tm * BF16_SUBLANES, LANES), F32)],
        compiler_params=_params(("parallel",)), name="mixout",
    )(*args)


def _route_kernel(seq, cap, jb, lg_ref, idx_ref, gate_ref, slot_scr, aff_scr, hit_scr):
    lg = lg_ref[0]
    ex = jnp.exp(lg - jnp.max(lg, axis=0, keepdims=True))
    aff = ex / jnp.sum(ex, axis=0, keepdims=True)
    bits = pltpu.bitcast(aff, jnp.int32)
    tok = lax.broadcasted_iota(jnp.int32, aff.shape, 1)
    capf = jnp.float32(cap)

    def count(mask):
        return jnp.sum(jnp.where(mask, 1.0, 0.0), axis=1, keepdims=True)

    thr = jnp.zeros((N_EXPERTS, 1), jnp.int32)
    for bit in range(30, -1, -1):
        cand = thr | jnp.int32(1 << bit)
        thr = jnp.where(count(bits >= cand) >= capf, cand, thr)
    gt = bits > thr
    eq = bits == thr
    need = capf - count(gt)
    cut = jnp.zeros((N_EXPERTS, 1), jnp.int32)
    for bit in range(max(seq.bit_length(), 1) - 1, -1, -1):
        cand = cut | jnp.int32(1 << bit)
        cut = jnp.where(count(eq & (tok < cand)) <= need, cand, cut)
    sel = gt | (eq & (tok < cut))

    r_i = lax.broadcasted_iota(jnp.int32, (LANES, LANES), 0)
    c_i = lax.broadcasted_iota(jnp.int32, (LANES, LANES), 1)
    tri = jnp.where(r_i < c_i, 1.0, 0.0).astype(BF16)
    self = jnp.where(sel, 1.0, 0.0)
    carry = jnp.zeros((N_EXPERTS, 1), F32)
    slots = []
    for blk in range(seq // LANES):
        seg = self[:, blk * LANES:(blk + 1) * LANES]
        excl = jnp.dot(seg.astype(BF16), tri, preferred_element_type=F32)
        slots.append(jnp.where(seg > 0.0, excl + carry, -1.0))
        carry = carry + jnp.sum(seg, axis=1, keepdims=True)
    slot_scr[...] = jnp.concatenate(slots, axis=1)
    aff_scr[...] = aff

    n_chunks = seq // LANES
    jcol = lax.broadcasted_iota(jnp.int32, (jb, LANES), 0).astype(F32)
    lane = lax.broadcasted_iota(jnp.int32, (jb, LANES), 1).astype(F32)

    n_blk = cap // jb
    lead_want = lax.broadcasted_iota(jnp.int32, (n_blk, LANES), 0).astype(F32) * float(jb)
    lead_lane = lax.broadcasted_iota(jnp.int32, (n_blk, LANES), 1).astype(F32)

    for e in range(N_EXPERTS):
        def lead_chunk(cb, acc, e=e):
            off = pl.multiple_of(cb * LANES, LANES)
            hit = slot_scr[e:e + 1, pl.ds(off, LANES)] == lead_want
            return acc + jnp.where(hit, lead_lane + off.astype(F32), 0.0)

        lead = lax.fori_loop(0, n_chunks, lead_chunk, jnp.zeros((n_blk, LANES), F32), unroll=min(n_chunks, 4))
        lead_tok = jnp.sum(lead, axis=1, keepdims=True).astype(jnp.int32)

        for blk in range(n_blk):
            j0 = blk * jb
            want = jcol + float(j0)
            first = lead_tok[blk, 0] // LANES
            last = lead_tok[blk + 1, 0] // LANES if blk + 1 < n_blk else n_chunks - 1

            def per_chunk(cb, carry, e=e, want=want):
                idx_acc, g_acc = carry
                off = pl.multiple_of(cb * LANES, LANES)
                hit = slot_scr[e:e + 1, pl.ds(off, LANES)] == want
                idx_acc = idx_acc + jnp.where(hit, lane + off.astype(F32), 0.0)
                g_acc = g_acc + jnp.where(hit, aff_scr[e:e + 1, pl.ds(off, LANES)], 0.0)
                return idx_acc, g_acc

            zero = jnp.zeros((jb, LANES), F32)
            idx_acc, g_acc = lax.fori_loop(first, last + 1, per_chunk, (zero, zero))
            hit_scr[0, j0:j0 + jb, :] = idx_acc
            hit_scr[1, j0:j0 + jb, :] = g_acc
        idx_ref[0, e] = jnp.sum(hit_scr[0], axis=1, keepdims=True).astype(jnp.int32)
        gate_ref[0, e] = jnp.sum(hit_scr[1], axis=1, keepdims=True)


def _route(logits_t, cap):
    b, e, seq = logits_t.shape
    jb = min(cap, 64)
    return pl.pallas_call(
        functools.partial(_route_kernel, seq, cap, jb),
        grid=(b,),
        in_specs=[pl.BlockSpec((1, e, seq), lambda i: (i, 0, 0))],
        out_specs=[pl.BlockSpec((1, e, cap, 1), lambda i: (i, 0, 0, 0)),
                   pl.BlockSpec((1, e, cap, 1), lambda i: (i, 0, 0, 0))],
        out_shape=[jax.ShapeDtypeStruct((b, e, cap, 1), jnp.int32), jax.ShapeDtypeStruct((b, e, cap, 1), F32)],
        scratch_shapes=[pltpu.VMEM((e, seq), F32), pltpu.VMEM((e, seq), F32), pltpu.VMEM((2, cap, LANES), F32)],
        compiler_params=_params(("parallel",)), name="route",
    )(logits_t)


def _moe_up_kernel(cap, d_model, pitch, idx_ref, hg_ref, wg_ref, wu_ref, hid_ref, xg_scr, xs_scr):
    b, e = pl.program_id(0), pl.program_id(1)

    def gather(expert, slot, j):
        t = idx_ref[(b * N_EXPERTS + expert) * cap + j]
        src = pl.multiple_of(t * BF16_SUBLANES, BF16_SUBLANES)
        tile = hg_ref[0, pl.ds(src, BF16_SUBLANES), :].astype(F32)
        xg_scr[slot, pl.ds(j, SUBLANES, stride=pitch), :] = tile[:SUBLANES]
        xg_scr[slot, pl.ds(SUBLANES * pitch + j, SUBLANES, stride=pitch), :] = tile[SUBLANES:]

    @pl.when(e == 0)
    def _():
        def first(j, _):
            gather(0, 0, j)
            return 0
        lax.fori_loop(0, cap, first, 0, unroll=8)

    slot = e % 2
    for c in range(d_model // LANES):
        xs_scr[:, c * LANES:(c + 1) * LANES] = xg_scr[slot, c * pitch:c * pitch + cap, :].astype(BF16)
    xs = xs_scr[...]
    g = jnp.dot(xs, wg_ref[0], preferred_element_type=F32)
    u = jnp.dot(xs, wu_ref[0], preferred_element_type=F32)
    hid_ref[0, 0] = (g * _sigmoid(g) * u).astype(BF16)
    nxt = jnp.minimum(e + 1, N_EXPERTS - 1)
    for j in range(cap):
        gather(nxt, 1 - slot, j)


def _moe_up(idx_flat, hg, wg, wu, *, cap):
    b, rows, _ = hg.shape
    n_e, d_model, d_ff = wg.shape
    pitch = cap + SUBLANES
    grid_spec = pltpu.PrefetchScalarGridSpec(
        num_scalar_prefetch=1, grid=(b, n_e),
        in_specs=[pl.BlockSpec((1, rows, LANES), lambda i, e, idx: (i, 0, 0), pipeline_mode=pl.Buffered(1)),
                  pl.BlockSpec((1, d_model, d_ff), lambda i, e, idx: (e, 0, 0)),
                  pl.BlockSpec((1, d_model, d_ff), lambda i, e, idx: (e, 0, 0))],
        out_specs=pl.BlockSpec((1, 1, cap, d_ff), lambda i, e, idx: (i, e, 0, 0)),
        scratch_shapes=[pltpu.VMEM((2, d_model // LANES * pitch, LANES), F32), pltpu.VMEM((cap, d_model), BF16)])
    return pl.pallas_call(
        functools.partial(_moe_up_kernel, cap, d_model, pitch),
        grid_spec=grid_spec,
        out_shape=jax.ShapeDtypeStruct((b, n_e, cap, d_ff), BF16),
        compiler_params=_params(("arbitrary", "arbitrary")), name="moe_up",
    )(idx_flat, hg, wg, wu)


def _moe_down_kernel(cap, half, pitch, group, idx_ref, hid_ref, gate_ref, wd_ref, acc_ref, og_scr):
    b, e = pl.program_id(0), pl.program_id(2)

    def project(slot):
        o = jnp.dot(hid_ref[0, 0], wd_ref[0], preferred_element_type=F32) * gate_ref[0]
        for c in range(half // LANES):
            og_scr[slot, c * pitch:c * pitch + cap, :] = o[:, c * LANES:(c + 1) * LANES]

    def scatter(expert, slot, j0):
        base = (b * N_EXPERTS + expert) * cap
        dsts = [pl.multiple_of(idx_ref[base + j0 + k] * SUBLANES, SUBLANES) for k in range(group)]
        vals = [acc_ref[0, 0, pl.ds(dsts[k], SUBLANES), :] + og_scr[slot, pl.ds(j0 + k, SUBLANES, stride=pitch), :]
                for k in range(group)]
        for k in range(group):
            acc_ref[0, 0, pl.ds(dsts[k], SUBLANES), :] = vals[k]

    slot = e % 2

    @pl.when(e == 0)
    def _():
        acc_ref[...] = jnp.zeros_like(acc_ref)
        project(slot)

    @pl.when(e > 0)
    def _():
        for jg in range(cap // group):
            scatter(e - 1, 1 - slot, jg * group)
        project(slot)

    @pl.when(e == N_EXPERTS - 1)
    def _():
        def flush(jg, _):
            scatter(e, slot, jg * group)
            return 0
        lax.fori_loop(0, cap // group, flush, 0)


def _moe_down(idx_flat, hid, gate, wd, *, seq):
    b, n_e, cap, d_ff = hid.shape
    d_model = wd.shape[2]
    half = d_model // 2
    pitch = cap + SUBLANES
    group = min(cap, 8)
    grid_spec = pltpu.PrefetchScalarGridSpec(
        num_scalar_prefetch=1, grid=(b, 2, n_e),
        in_specs=[pl.BlockSpec((1, 1, cap, d_ff), lambda i, p, e, idx: (i, e, 0, 0)),
                  pl.BlockSpec((1, cap, 1), lambda i, p, e, idx: (i * n_e + e, 0, 0)),
                  pl.BlockSpec((1, d_ff, half), lambda i, p, e, idx: (e, 0, p))],
        out_specs=pl.BlockSpec((1, 1, seq * SUBLANES, LANES), lambda i, p, e, idx: (i, p, 0, 0)),
        scratch_shapes=[pltpu.VMEM((2, half // LANES * pitch, LANES), F32)])
    return pl.pallas_call(
        functools.partial(_moe_down_kernel, cap, half, pitch, group),
        grid_spec=grid_spec,
        out_shape=jax.ShapeDtypeStruct((b, 2, seq * SUBLANES, LANES), F32),
        compiler_params=_params(("arbitrary", "arbitrary", "arbitrary")), name="moe_down",
    )(idx_flat, hid, gate, wd)


def _final_kernel(tm, x_ref, moe_ref, g2_ref, g_ref, y_ref):
    x = _moe_residual(x_ref[...], moe_ref, g2_ref[0], tm)
    y_ref[...] = _rms(x, g_ref[...])


def _final(x, moe, mod_prev, g, *, tm, tiles_per_mod):
    t, d = x.shape
    tiles_per_row = moe.shape[2] // (tm * SUBLANES)
    return pl.pallas_call(
        functools.partial(_final_kernel, tm),
        grid=(t // tm,),
        in_specs=[pl.BlockSpec((tm, d), lambda i: (i, 0)),
                  pl.BlockSpec((1, 2, tm * SUBLANES, LANES), lambda i: (i // tiles_per_row, 0, i % tiles_per_row, 0)),
                  pl.BlockSpec((1, 1, d), lambda i: (i // tiles_per_mod, 0, 5)),
                  pl.BlockSpec(g.shape, lambda i: (0, 0))],
        out_specs=pl.BlockSpec((tm, d), lambda i: (i, 0)),
        out_shape=jax.ShapeDtypeStruct((t, d), F32),
        compiler_params=_params(("parallel",)), name="final_norm",
    )(x, moe, mod_prev, g)


def _rope_swap_perm():
    q = QK_ROPE // 4
    return jnp.array(list(range(q, 2 * q)) + list(range(0, q)) + list(range(3 * q, 4 * q)) + list(range(2 * q, 3 * q)),
                     jnp.int32)


def _rope_table(n_tokens):
    rows = n_tokens // GRID_W
    row = jnp.repeat(jnp.arange(rows, dtype=F32), GRID_W)
    col = jnp.tile(jnp.arange(GRID_W, dtype=F32), rows)
    n_pairs = QK_ROPE // 4
    inv = ROPE_BASE ** (-jnp.arange(n_pairs, dtype=F32) / n_pairs)
    ang_r = row[:, None] * inv
    ang_c = col[:, None] * inv
    cr, sr, cc, sn = jnp.cos(ang_r), jnp.sin(ang_r), jnp.cos(ang_c), jnp.sin(ang_c)
    return jnp.concatenate([cr, cr, cc, cc, -sr, sr, -sn, sn], axis=1)


def _layer_weights(l, w):
    d = w["w_in"].shape[1]
    q_lora = w["q_norm_g"].shape[1]
    kv_lora = w["kv_norm_g"].shape[1]
    perm = _rope_swap_perm()
    w_in = w["w_in"][l]
    c1 = q_lora + kv_lora
    kr_w = w_in[:, c1:c1 + QK_ROPE]
    w_in_p = jnp.concatenate([w_in[:, :c1 + QK_ROPE], kr_w[:, perm], w_in[:, c1 + QK_ROPE:]], axis=1).astype(BF16)
    w_uq = w["w_uq"][l]
    rope_w = w_uq[:, :, QK_NOPE:]
    w_uq_p = jnp.concatenate([w_uq, rope_w[:, :, perm]], axis=2).reshape(q_lora, N_HEADS * HEAD_SLAB).astype(BF16)
    w_ukv = w["w_ukv"][l]
    w_uk_p = w_ukv[:, :, :QK_NOPE].reshape(kv_lora, -1).astype(BF16)
    w_vt_p = w_ukv[:, :, QK_NOPE:].reshape(kv_lora, -1).T.astype(BF16)
    return dict(
        w_in=w_in_p, q_norm_g=w["q_norm_g"][l][None], w_uq=w_uq_p, kv_norm_g=w["kv_norm_g"][l][None], w_uk=w_uk_p, w_vt=w_vt_p,
        norm1_g=w["norm1_g"][l][None], norm2_g=w["norm2_g"][l][None],
        sc_conv_w=w["sc_conv_w"][l], cf_conv_w=w["cf_conv_w"][l], cf_conv_b=w["cf_conv_b"][l][None],
        cf_ln_g=w["cf_ln_g"][l][None], cf_ln_b=w["cf_ln_b"][l][None],
        w_o=w["w_o"][l].astype(BF16), router_wt=w["router_w"][l].T,
        wg=w["exp_w_gate"][l].astype(BF16), wu=w["exp_w_up"][l].astype(BF16), wd=w["exp_w_down"][l].astype(BF16))


def _run_stream(x0, mods, lws, final_g, *, batch, seq, tab, cache_k, cache_v, merge_moe_rows, tm, tq, tk, n_hd, ahead):
    t, d = x0.shape
    n_mod = mods[0].shape[0]
    tiles_per_mod = t // n_mod // tm
    tiles_per_seq = seq // tm
    cap = max(1, EC_CAPACITY * seq // N_EXPERTS)
    moe_b, moe_seq = (1, t) if merge_moe_rows else (batch, seq)
    moe_cap = cap * (batch // moe_b)
    x, moe = x0, None
    ckvs, krs = [], []
    for l, lw in enumerate(lws):
        x, q, k, v, cv, ckvn, kr = _inproj(x, moe, mods[l - 1] if l else None, mods[l], lw["norm1_g"], tab, lw,
                                           tm=tm, tiles_per_mod=tiles_per_mod, tiles_per_seq=tiles_per_seq)
        ckvs.append(ckvn)
        krs.append(kr)
        kc = None if cache_k is None else cache_k[l]
        vc = None if cache_v is None else cache_v[l]
        o_att = _attention(q, k, v, kc, vc, batch=batch, seq=seq, tq=tq, tk=tk, n_hd=n_hd, ahead=ahead)
        x, hg, logits_t = _mixout(o_att, cv, x, mods[l], lw, tm=tm, tiles_per_mod=tiles_per_mod,
                                  tiles_per_seq=tiles_per_seq, route_batch=batch, route_len=seq)
        idx, gate = _route(logits_t, cap)
        if moe_b != batch:
            offs = (jnp.arange(batch, dtype=jnp.int32) * seq)[:, None, None, None]
            idx = jnp.transpose(idx + offs, (1, 0, 2, 3)).reshape(1, N_EXPERTS, moe_cap, 1)
            gate = jnp.transpose(gate, (1, 0, 2, 3)).reshape(1, N_EXPERTS, moe_cap, 1)
        idx_flat = idx.reshape(-1)
        gate = gate.reshape(moe_b * N_EXPERTS, moe_cap, 1)
        hid = _moe_up(idx_flat, hg.reshape(moe_b, moe_seq * BF16_SUBLANES, LANES), lw["wg"], lw["wu"], cap=moe_cap)
        moe = _moe_down(idx_flat, hid, gate, lw["wd"], seq=moe_seq)
    y = _final(x, moe, mods[-1], final_g, tm=tm, tiles_per_mod=tiles_per_mod)
    return y, ckvs, krs


def kernel(x_prompt, x_sample, cache_ckv, cache_krope, c, c_ctx, ada_w, ada_b, norm1_g, norm2_g, w_in, q_norm_g, w_uq,
           kv_norm_g, w_ukv, sc_conv_w, cf_conv_w, cf_conv_b, cf_ln_g, cf_ln_b, w_o, router_w, exp_w_gate, exp_w_up,
           exp_w_down, final_norm_g):
    batch, seq, d = x_prompt.shape
    dec_batch, dec_seq, _ = x_sample.shape
    depth = ada_w.shape[0]
    past = cache_ckv.shape[2]
    w = dict(norm1_g=norm1_g, norm2_g=norm2_g, w_in=w_in, q_norm_g=q_norm_g, w_uq=w_uq, kv_norm_g=kv_norm_g, w_ukv=w_ukv,
             sc_conv_w=sc_conv_w, cf_conv_w=cf_conv_w, cf_conv_b=cf_conv_b, cf_ln_g=cf_ln_g, cf_ln_b=cf_ln_b, w_o=w_o,
             router_w=router_w, exp_w_gate=exp_w_gate, exp_w_up=exp_w_up, exp_w_down=exp_w_down)
    lws = [_layer_weights(l, w) for l in range(depth)]

    n_rows = -(-(1 + dec_batch) // SUBLANES) * SUBLANES
    c_all = jnp.zeros((n_rows, d), F32).at[0].set(c_ctx).at[1:1 + dec_batch].set(c)
    mod = _ada_mod(c_all, ada_w, ada_b)
    mods_p = [mod[l, 0:1][:, None, :] for l in range(depth)]
    mods_s = [mod[l, 1:1 + dec_batch][:, None, :] for l in range(depth)]

    tm = 256
    final_g = final_norm_g[None]

    ident = jnp.concatenate([jnp.ones((tm, QK_ROPE), F32), jnp.zeros((tm, QK_ROPE), F32)], axis=1)
    y_p, ckvs, krs = _run_stream(x_prompt.reshape(batch * seq, d), mods_p, lws, final_g, batch=batch, seq=seq, tab=ident,
                                 cache_k=None, cache_v=None, merge_moe_rows=True, tm=tm, tq=min(seq, 256), tk=min(seq, 512), n_hd=N_HEADS, ahead=1)

    ckv_c = jnp.transpose(cache_ckv, (1, 0, 2, 3))
    kr_c = jnp.transpose(cache_krope, (1, 0, 2, 3))
    kr_c = jnp.pad(kr_c, ((0, 0), (0, 0), (0, 0), (0, LANES - kr_c.shape[-1])))
    ck, cvv = _cache_kv(ckv_c, kr_c, jnp.stack([lw["w_uk"] for lw in lws]), jnp.stack([lw["w_vt"] for lw in lws]))
    y_s, _, _ = _run_stream(x_sample.reshape(dec_batch * dec_seq, d), mods_s, lws, final_g, batch=dec_batch, seq=dec_seq,
                            tab=_rope_table(dec_seq), cache_k=ck, cache_v=cvv, merge_moe_rows=False, tm=tm, tq=256, tk=min(512, dec_seq // 4), n_hd=4, ahead=3)

    new_ckv = jnp.stack([a.reshape(batch, seq, -1) for a in ckvs], axis=1)
    new_kr = jnp.stack([a.reshape(batch, seq, -1) for a in krs], axis=1)
    return (y_p.reshape(batch, seq, d), y_s.reshape(dec_batch, dec_seq, d), new_ckv, new_kr)
```

```python
import functools

import jax
import jax.numpy as jnp
from jax import lax
from jax.experimental import pallas as pl
from jax.experimental.pallas import tpu as pltpu

LANES = 128
SUBLANES = 8
BF16_SUBLANES = 16
VMEM_LIMIT_BYTES = 56 * 1024 * 1024

N_HEADS = 8
QK_NOPE = 128
QK_ROPE = 64
V_DIM = 128
HEAD_SLAB = 256
N_EXPERTS = 16
EC_CAPACITY = 2
GRID_W = 64
SC_WIDTH = 3
CF_WIDTH = 31
CF_PAD = (CF_WIDTH - 1) // 2
HALO = 16
CONV_ROWS = 64
ROPE_BASE = 10000.0
NORM_EPS = 1e-6
ATTN_SCALE = (QK_NOPE + QK_ROPE) ** -0.5
LOG2_E = 1.4426950408889634
Q_SCALE = ATTN_SCALE * LOG2_E

F32 = jnp.float32
BF16 = jnp.bfloat16


def _params(sem):
    return pltpu.CompilerParams(dimension_semantics=sem, vmem_limit_bytes=VMEM_LIMIT_BYTES)


def _sigmoid(x):
    return 1.0 / (1.0 + jnp.exp(-x))


def _rms(x, g):
    return x * lax.rsqrt(jnp.mean(x * x, axis=-1, keepdims=True) + NORM_EPS) * g


def _split3_dot(a, b, dims):
    a_hi = a.astype(BF16)
    a_lo = (a - a_hi.astype(F32)).astype(BF16)
    b_hi = b.astype(BF16)
    b_lo = (b - b_hi.astype(F32)).astype(BF16)
    dot = functools.partial(lax.dot_general, dimension_numbers=(dims, ((), ())), preferred_element_type=F32)
    return dot(a_hi, b_hi) + dot(a_lo, b_hi) + dot(a_hi, b_lo)


def _ada_kernel(c_ref, w_ref, b_ref, o_ref):
    cv = c_ref[...]
    s = cv * _sigmoid(cv)
    o_ref[0] = _split3_dot(s, w_ref[0], ((1,), (0,))) + b_ref[0]


def _ada_mod(c_all, ada_w, ada_b):
    depth, d, n6 = ada_w.shape
    rows = c_all.shape[0]
    tn = 1024
    return pl.pallas_call(
        _ada_kernel,
        grid=(depth, n6 // tn),
        in_specs=[pl.BlockSpec((rows, d), lambda l, j: (0, 0)),
                  pl.BlockSpec((1, d, tn), lambda l, j: (l, 0, j)),
                  pl.BlockSpec((1, 1, tn), lambda l, j: (l, 0, j))],
        out_specs=pl.BlockSpec((1, rows, tn), lambda l, j: (l, 0, j)),
        out_shape=jax.ShapeDtypeStruct((depth, rows, n6), F32),
        compiler_params=_params(("parallel", "parallel")),
        name="ada_mod",
    )(c_all, ada_w, ada_b.reshape(depth, 1, n6))


def _moe_residual(x, moe_ref, g2, tm):
    chunks = []
    for p in range(2):
        for c in range(SUBLANES):
            chunks.append(moe_ref[0, p, pl.ds(c, tm, stride=SUBLANES), :])
    return x + g2 * jnp.concatenate(chunks, axis=1)


def _rope_pair(v, tab):
    prod = v * tab
    r = prod + pltpu.roll(prod, QK_ROPE, 1)
    lane = lax.broadcasted_iota(jnp.int32, r.shape, 1)
    return jnp.where(lane < QK_ROPE, r, 0.0)


def _inproj_kernel(has_moe, tm, d_model, q_lora, kv_lora, *refs):
    if has_moe:
        (x_ref, moe_ref, g2_ref, shsc_ref, n1_ref, tab_ref, win_ref, qg_ref, wuq_ref, kvg_ref, wuk_ref, wvt_ref,
         xo_ref, q_ref, k_ref, vt_ref, cv_ref, ckv_ref, kr_ref) = refs
        x = _moe_residual(x_ref[...], moe_ref, g2_ref[0], tm)
        xo_ref[...] = x
    else:
        (x_ref, shsc_ref, n1_ref, tab_ref, win_ref, qg_ref, wuq_ref, kvg_ref, wuk_ref, wvt_ref,
         q_ref, k_ref, vt_ref, cv_ref, ckv_ref, kr_ref) = refs
        x = x_ref[...]
    sh1 = shsc_ref[0, :, :d_model]
    sc1 = shsc_ref[0, :, d_model:]
    h = _rms(x, n1_ref[...]) * (1.0 + sc1) + sh1
    u = jnp.dot(h.astype(BF16), win_ref[...], preferred_element_type=F32)
    tab = tab_ref[...]

    cqn = _rms(u[:, :q_lora], qg_ref[...])
    qf = jnp.dot(cqn.astype(BF16), wuq_ref[...], preferred_element_type=F32)
    for hd in range(N_HEADS):
        o = hd * HEAD_SLAB
        q_ref[:, o:o + QK_NOPE] = (qf[:, o:o + QK_NOPE] * Q_SCALE).astype(BF16)
        q_ref[:, o + QK_NOPE:o + HEAD_SLAB] = (_rope_pair(qf[:, o + QK_NOPE:o + HEAD_SLAB], tab) * Q_SCALE).astype(BF16)

    c0 = q_lora
    ckvn = _rms(u[:, c0:c0 + kv_lora], kvg_ref[...])
    ckv_ref[...] = ckvn
    ckvb = ckvn.astype(BF16)
    knf = jnp.dot(ckvb, wuk_ref[...], preferred_element_type=F32)
    vt_ref[0] = lax.dot_general(wvt_ref[...], ckvb, (((1,), (1,)), ((), ())), preferred_element_type=F32).astype(BF16)
    c1 = c0 + kv_lora
    krv = u[:, c1:c1 + LANES]
    kr_ref[...] = krv[:, :QK_ROPE]
    kr2 = _rope_pair(krv, tab).astype(BF16)
    for hd in range(N_HEADS):
        o = hd * HEAD_SLAB
        k_ref[:, o:o + QK_NOPE] = knf[:, hd * QK_NOPE:(hd + 1) * QK_NOPE].astype(BF16)
        k_ref[:, o + QK_NOPE:o + HEAD_SLAB] = kr2
    cv_ref[...] = u[:, c1 + LANES:]


def _inproj(x, moe, mod_prev, mod, n1, tab, lw, *, tm, tiles_per_mod, tiles_per_seq):
    t, d = x.shape
    nt = t // tm
    p_in = lw["w_in"].shape[1]
    q_lora = lw["q_norm_g"].shape[1]
    kv_lora = lw["kv_norm_g"].shape[1]
    n_cv = p_in - q_lora - kv_lora - LANES
    has_moe = moe is not None
    tiles_per_row = None if moe is None else moe.shape[2] // (tm * SUBLANES)

    full = lambda a: pl.BlockSpec(a.shape, lambda i: (0,) * a.ndim)
    row = lambda w: pl.BlockSpec((tm, w), lambda i: (i, 0))
    in_specs = [row(d)]
    args = [x]
    if has_moe:
        in_specs += [pl.BlockSpec((1, 2, tm * SUBLANES, LANES), lambda i: (i // tiles_per_row, 0, i % tiles_per_row, 0)),
                     pl.BlockSpec((1, 1, d), lambda i: (i // tiles_per_mod, 0, 5))]
        args += [moe, mod_prev]
    in_specs += [pl.BlockSpec((1, 1, 2 * d), lambda i: (i // tiles_per_mod, 0, 0)),
                 full(n1),
                 pl.BlockSpec((tm, LANES), lambda i: (i % tiles_per_seq, 0)),
                 full(lw["w_in"]), full(lw["q_norm_g"]), full(lw["w_uq"]), full(lw["kv_norm_g"]), full(lw["w_uk"]), full(lw["w_vt"])]
    args += [mod, n1, tab, lw["w_in"], lw["q_norm_g"], lw["w_uq"], lw["kv_norm_g"], lw["w_uk"], lw["w_vt"]]

    out_specs, out_shape = [], []
    if has_moe:
        out_specs.append(row(d))
        out_shape.append(jax.ShapeDtypeStruct((t, d), F32))
    seq = tiles_per_seq * tm
    for w, dt in ((N_HEADS * HEAD_SLAB, BF16), (N_HEADS * HEAD_SLAB, BF16), (None, BF16),
                  (n_cv, F32), (kv_lora, F32), (QK_ROPE, F32)):
        if w is None:
            out_specs.append(pl.BlockSpec((1, N_HEADS * V_DIM, tm), lambda i: (i // tiles_per_seq, 0, i % tiles_per_seq)))
            out_shape.append(jax.ShapeDtypeStruct((t // seq, N_HEADS * V_DIM, seq), dt))
        else:
            out_specs.append(row(w))
            out_shape.append(jax.ShapeDtypeStruct((t, w), dt))

    outs = pl.pallas_call(
        functools.partial(_inproj_kernel, has_moe, tm, d, q_lora, kv_lora),
        grid=(nt,), in_specs=in_specs, out_specs=out_specs, out_shape=out_shape,
        compiler_params=_params(("parallel",)), name="inproj",
    )(*args)
    if not has_moe:
        outs = [x] + list(outs)
    return outs


def _cachekv_kernel(ckv_ref, kr_ref, wuk_ref, wvt_ref, k_ref, vt_ref):
    ckvb = ckv_ref[0, 0].astype(BF16)
    knf = jnp.dot(ckvb, wuk_ref[0], preferred_element_type=F32)
    kr2 = kr_ref[0, 0].astype(BF16)
    for hd in range(N_HEADS):
        o = hd * HEAD_SLAB
        k_ref[0, 0, :, o:o + QK_NOPE] = knf[:, hd * QK_NOPE:(hd + 1) * QK_NOPE].astype(BF16)
        k_ref[0, 0, :, o + QK_NOPE:o + HEAD_SLAB] = kr2
    vt_ref[0, 0] = lax.dot_general(wvt_ref[0], ckvb, (((1,), (1,)), ((), ())), preferred_element_type=F32).astype(BF16)


def _cache_kv(ckv_c, kr_c, w_uk_all, w_vt_all):
    depth, b, past, kv_lora = ckv_c.shape
    nk = N_HEADS * HEAD_SLAB
    nv = N_HEADS * V_DIM
    return pl.pallas_call(
        _cachekv_kernel,
        grid=(depth, b),
        in_specs=[pl.BlockSpec((1, 1, past, kv_lora), lambda l, i: (l, i, 0, 0)),
                  pl.BlockSpec((1, 1, past, LANES), lambda l, i: (l, i, 0, 0)),
                  pl.BlockSpec((1, kv_lora, N_HEADS * QK_NOPE), lambda l, i: (l, 0, 0)),
                  pl.BlockSpec((1, nv, kv_lora), lambda l, i: (l, 0, 0))],
        out_specs=[pl.BlockSpec((1, 1, past, nk), lambda l, i: (l, i, 0, 0)),
                   pl.BlockSpec((1, 1, nv, past), lambda l, i: (l, i, 0, 0))],
        out_shape=[jax.ShapeDtypeStruct((depth, b, past, nk), BF16), jax.ShapeDtypeStruct((depth, b, nv, past), BF16)],
        compiler_params=_params(("parallel", "parallel")), name="cache_kv",
    )(ckv_c, kr_c, w_uk_all, w_vt_all)


def _attn_kernel(has_ctx, n_hd, ahead, lk, tk, *refs):
    if has_ctx:
        q_ref, k_ref, vt_ref, kc_ref, vct_ref, o_ref, s_scr, sc_scr = refs
    else:
        q_ref, k_ref, vt_ref, o_ref, s_scr = refs
    tq = q_ref.shape[0]
    n = lk // tk
    qs = [q_ref[:, g * HEAD_SLAB:(g + 1) * HEAD_SLAB] for g in range(n_hd)]

    def scores(g, kc, s_ref):
        s = lax.dot_general(kc, qs[g], (((1,), (1,)), ((), ())), preferred_element_type=F32)
        s_ref[...] = s
        return jnp.max(s, axis=0, keepdims=True)

    def k_chunk(g, c):
        return k_ref[c * tk:(c + 1) * tk, g * HEAD_SLAB:(g + 1) * HEAD_SLAB]

    def vt_chunk(g, c):
        return vt_ref[0, g * V_DIM:(g + 1) * V_DIM, c * tk:(c + 1) * tk]

    def accumulate(state, s_ref, smax, vt):
        m, l, acc = state
        m_new = jnp.maximum(m, smax)
        alpha = jnp.exp2(m - m_new)
        p = jnp.exp2(s_ref[...] - m_new)
        l = alpha * l + jnp.sum(p, axis=0, keepdims=True)
        acc = alpha * acc + jnp.dot(vt, p.astype(BF16), preferred_element_type=F32)
        return m_new, l, acc

    total = n + (1 if has_ctx else 0)
    n_slots = s_scr.shape[1]

    def s_of(g, c):
        return s_scr.at[g, c % n_slots] if c < n else sc_scr.at[g]

    def issue(g, c):
        kc = k_chunk(g, c) if c < n else kc_ref[0, :, g * HEAD_SLAB:(g + 1) * HEAD_SLAB]
        return scores(g, kc, s_of(g, c))

    def vt_of(g, c):
        return vt_chunk(g, c) if c < n else vct_ref[0, g * V_DIM:(g + 1) * V_DIM, :]

    states = [(jnp.full((1, tq), -jnp.inf, F32), jnp.zeros((1, tq), F32), jnp.zeros((V_DIM, tq), F32))] * n_hd
    smax = {}
    for c in range(min(ahead, total)):
        for g in range(n_hd):
            smax[g, c] = issue(g, c)
    for c in range(total):
        for g in range(n_hd):
            if c + ahead < total:
                smax[g, c + ahead] = issue(g, c + ahead)
            states[g] = accumulate(states[g], s_of(g, c), smax.pop((g, c)), vt_of(g, c))
    for g in range(n_hd):
        _, l, acc = states[g]
        o_ref[:, g * V_DIM:(g + 1) * V_DIM] = jnp.transpose(acc / l).astype(BF16)


def _attention(q, k, vt, kc, vct, *, batch, seq, tq, tk, n_hd, ahead):
    t = q.shape[0]
    nq = seq // tq
    has_ctx = kc is not None
    in_specs = [pl.BlockSpec((tq, n_hd * HEAD_SLAB), lambda b, h, i: (b * nq + i, h)),
                pl.BlockSpec((seq, n_hd * HEAD_SLAB), lambda b, h, i: (b, h)),
                pl.BlockSpec((1, n_hd * V_DIM, seq), lambda b, h, i: (b, h, 0))]
    args = [q, k, vt]
    scratch = [pltpu.VMEM((n_hd, ahead + 1, tk, tq), F32)]
    if has_ctx:
        past = kc.shape[1]
        in_specs += [pl.BlockSpec((1, past, n_hd * HEAD_SLAB), lambda b, h, i: (b, 0, h)),
                     pl.BlockSpec((1, n_hd * V_DIM, past), lambda b, h, i: (b, h, 0))]
        args += [kc, vct]
        scratch += [pltpu.VMEM((n_hd, past, tq), F32)]
    return pl.pallas_call(
        functools.partial(_attn_kernel, has_ctx, n_hd, ahead, seq, tk),
        grid=(batch, N_HEADS // n_hd, nq), in_specs=in_specs,
        out_specs=pl.BlockSpec((tq, n_hd * V_DIM), lambda b, h, i: (b * nq + i, h)),
        out_shape=jax.ShapeDtypeStruct((t, N_HEADS * V_DIM), BF16),
        scratch_shapes=scratch,
        compiler_params=_params(("parallel", "parallel", "arbitrary")), name="attention",
    )(*args)


def _mixout_kernel(tm, tiles_per_seq, d_sc, d_cf,
                   o_ref, cv_ref, cvp_ref, cvn_ref, x_ref, g1_ref, sh2_ref, sc2_ref,
                   scw_ref, cfw_ref, cfb_ref, lng_ref, lnb_ref, wo_ref, n2_ref, rwt_ref,
                   xo_ref, hg_ref, lg_ref, ext_sc, ext_cf, ph_scr, z_scr, hg_scr):
    i = pl.program_id(0)
    first = (i % tiles_per_seq) == 0
    last = (i % tiles_per_seq) == tiles_per_seq - 1
    o_scx, o_scb, o_scc, o_cfa, o_cfg = 0, d_sc, 2 * d_sc, 3 * d_sc, 3 * d_sc + d_cf

    def sc_in(ref):
        return ref[:, o_scc:o_scc + d_sc] * ref[:, o_scx:o_scx + d_sc]

    def cf_in(ref):
        return ref[:, o_cfa:o_cfa + d_cf] * _sigmoid(ref[:, o_cfg:o_cfg + d_cf])

    ext_sc[0:HALO, :] = jnp.where(first, 0.0, sc_in(cvp_ref))
    ext_sc[HALO:HALO + tm, :] = sc_in(cv_ref)
    ext_sc[HALO + tm:, :] = jnp.where(last, 0.0, sc_in(cvn_ref))
    ext_cf[0:HALO, :] = jnp.where(first, 0.0, cf_in(cvp_ref))
    ext_cf[HALO:HALO + tm, :] = cf_in(cv_ref)
    ext_cf[HALO + tm:, :] = jnp.where(last, 0.0, cf_in(cvn_ref))

    sc_pad = (SC_WIDTH - 1) // 2
    y_sc = jnp.zeros((tm, d_sc), F32)
    for kk in range(SC_WIDTH):
        y_sc = y_sc + scw_ref[kk:kk + 1, :] * ext_sc[pl.ds(HALO - sc_pad + kk, tm), :]
    y_sc = cv_ref[:, o_scb:o_scb + d_sc] * y_sc

    first_row = HALO - CF_PAD
    span = tm + 2 * HALO - SUBLANES
    for phase in range(1, SUBLANES):
        ph_scr[phase - 1, 0:span, :] = ext_cf[pl.ds(phase, span), :]
    for ct in range(d_cf // LANES):
        cs = slice(ct * LANES, (ct + 1) * LANES)
        for rb in range(tm // CONV_ROWS):
            acc = jnp.zeros((CONV_ROWS, LANES), F32)
            for kk in range(CF_WIDTH):
                phase = (first_row + kk) % SUBLANES
                base = first_row + kk - phase + rb * CONV_ROWS
                src = ext_cf if phase == 0 else ph_scr.at[phase - 1]
                acc = acc + cfw_ref[kk:kk + 1, cs] * src[base:base + CONV_ROWS, cs]
            z_scr[rb * CONV_ROWS:(rb + 1) * CONV_ROWS, cs] = acc
    z = z_scr[...] + cfb_ref[...]
    mu = jnp.mean(z, axis=-1, keepdims=True)
    zc = z - mu
    var = jnp.mean(zc * zc, axis=-1, keepdims=True)
    z = zc * lax.rsqrt(var + NORM_EPS) * lng_ref[...] + lnb_ref[...]
    z = z * _sigmoid(z)

    mixed = jnp.concatenate([o_ref[...], y_sc.astype(BF16), z.astype(BF16)], axis=1)
    y = jnp.dot(mixed, wo_ref[...], preferred_element_type=F32)
    x = x_ref[...] + g1_ref[0] * y
    xo_ref[...] = x
    h2 = _rms(x, n2_ref[...]) * (1.0 + sc2_ref[0]) + sh2_ref[0]

    lg_ref[0] = _split3_dot(rwt_ref[...], h2, ((1,), (1,)))

    d_model = h2.shape[1]
    for c in range(d_model // LANES):
        hg_scr[pl.ds(c, tm, stride=BF16_SUBLANES), :] = h2[:, c * LANES:(c + 1) * LANES]
    hg_ref[...] = hg_scr[...].astype(BF16)


def _mixout(o_att, cv, x, mod, lw, *, tm, tiles_per_mod, tiles_per_seq, route_batch, route_len):
    t, d = x.shape
    nt = t // tm
    n_cv = cv.shape[1]
    d_sc = lw["sc_conv_w"].shape[1]
    d_cf = lw["cf_conv_w"].shape[1]
    hb = tm // HALO
    n_halo = t // HALO
    tiles_per_route = route_len // tm
    full = lambda a: pl.BlockSpec(a.shape, lambda i: (0,) * a.ndim)
    row = lambda w: pl.BlockSpec((tm, w), lambda i: (i, 0))
    modspec = lambda blk: pl.BlockSpec((1, 1, d), lambda i: (i // tiles_per_mod, 0, blk))
    in_specs = [row(o_att.shape[1]), row(n_cv),
                pl.BlockSpec((HALO, n_cv), lambda i: (jnp.maximum(i * hb - 1, 0), 0)),
                pl.BlockSpec((HALO, n_cv), lambda i: (jnp.minimum((i + 1) * hb, n_halo - 1), 0)),
                row(d), modspec(2), modspec(3), modspec(4),
                full(lw["sc_conv_w"]), full(lw["cf_conv_w"]), full(lw["cf_conv_b"]), full(lw["cf_ln_g"]),
                full(lw["cf_ln_b"]), full(lw["w_o"]), full(lw["norm2_g"]), full(lw["router_wt"])]
    args = [o_att, cv, cv, cv, x, mod, mod, mod, lw["sc_conv_w"], lw["cf_conv_w"], lw["cf_conv_b"], lw["cf_ln_g"],
            lw["cf_ln_b"], lw["w_o"], lw["norm2_g"], lw["router_wt"]]
    out_specs = [row(d),
                 pl.BlockSpec((tm * BF16_SUBLANES, LANES), lambda i: (i, 0)),
                 pl.BlockSpec((1, N_EXPERTS, tm), lambda i: (i // tiles_per_route, 0, i % tiles_per_route))]
    out_shape = [jax.ShapeDtypeStruct((t, d), F32),
                 jax.ShapeDtypeStruct((t * BF16_SUBLANES, LANES), BF16),
                 jax.ShapeDtypeStruct((route_batch, N_EXPERTS, route_len), F32)]
    return pl.pallas_call(
        functools.partial(_mixout_kernel, tm, tiles_per_seq, d_sc, d_cf),
        grid=(nt,), in_specs=in_specs, out_specs=out_specs, out_shape=out_shape,
        scratch_shapes=[pltpu.VMEM((tm + 2 * HALO, d_sc), F32), pltpu.VMEM((tm + 2 * HALO, d_cf), F32),
                        pltpu.VMEM((SUBLANES - 1, tm + 2 * HALO, d_cf), F32), pltpu.VMEM((tm, d_cf), F32),
                        pltpu.VMEM((tm * BF16_SUBLANES, LANES), F32)],
        compiler_params=_params(("parallel",)), name="mixout",
    )(*args)


def _route_kernel(seq, cap, jb, lg_ref, idx_ref, gate_ref, slot_scr, aff_scr, hit_scr):
    lg = lg_ref[0]
    ex = jnp.exp(lg - jnp.max(lg, axis=0, keepdims=True))
    aff = ex / jnp.sum(ex, axis=0, keepdims=True)
    bits = pltpu.bitcast(aff, jnp.int32)
    tok = lax.broadcasted_iota(jnp.int32, aff.shape, 1)
    capf = jnp.float32(cap)

    def count(mask):
        return jnp.sum(jnp.where(mask, 1.0, 0.0), axis=1, keepdims=True)

    thr = jnp.zeros((N_EXPERTS, 1), jnp.int32)
    for bit in range(30, -1, -1):
        cand = thr | jnp.int32(1 << bit)
        thr = jnp.where(count(bits >= cand) >= capf, cand, thr)
    gt = bits > thr
    eq = bits == thr
    need = capf - count(gt)
    cut = jnp.zeros((N_EXPERTS, 1), jnp.int32)
    for bit in range(max(seq.bit_length(), 1) - 1, -1, -1):
        cand = cut | jnp.int32(1 << bit)
        cut = jnp.where(count(eq & (tok < cand)) <= need, cand, cut)
    sel = gt | (eq & (tok < cut))

    r_i = lax.broadcasted_iota(jnp.int32, (LANES, LANES), 0)
    c_i = lax.broadcasted_iota(jnp.int32, (LANES, LANES), 1)
    tri = jnp.where(r_i < c_i, 1.0, 0.0).astype(BF16)
    self = jnp.where(sel, 1.0, 0.0)
    carry = jnp.zeros((N_EXPERTS, 1), F32)
    slots = []
    for blk in range(seq // LANES):
        seg = self[:, blk * LANES:(blk + 1) * LANES]
        excl = jnp.dot(seg.astype(BF16), tri, preferred_element_type=F32)
        slots.append(jnp.where(seg > 0.0, excl + carry, -1.0))
        carry = carry + jnp.sum(seg, axis=1, keepdims=True)
    slot_scr[...] = jnp.concatenate(slots, axis=1)
    aff_scr[...] = aff

    n_chunks = seq // LANES
    jcol = lax.broadcasted_iota(jnp.int32, (jb, LANES), 0).astype(F32)
    lane = lax.broadcasted_iota(jnp.int32, (jb, LANES), 1).astype(F32)

    n_blk = cap // jb
    lead_want = lax.broadcasted_iota(jnp.int32, (n_blk, LANES), 0).astype(F32) * float(jb)
    lead_lane = lax.broadcasted_iota(jnp.int32, (n_blk, LANES), 1).astype(F32)

    for e in range(N_EXPERTS):
        def lead_chunk(cb, acc, e=e):
            off = pl.multiple_of(cb * LANES, LANES)
            hit = slot_scr[e:e + 1, pl.ds(off, LANES)] == lead_want
            return acc + jnp.where(hit, lead_lane + off.astype(F32), 0.0)

        lead = lax.fori_loop(0, n_chunks, lead_chunk, jnp.zeros((n_blk, LANES), F32), unroll=min(n_chunks, 4))
        lead_tok = jnp.sum(lead, axis=1, keepdims=True).astype(jnp.int32)

        for blk in range(n_blk):
            j0 = blk * jb
            want = jcol + float(j0)
            first = lead_tok[blk, 0] // LANES
            last = lead_tok[blk + 1, 0] // LANES if blk + 1 < n_blk else n_chunks - 1

            def per_chunk(cb, carry, e=e, want=want):
                idx_acc, g_acc = carry
                off = pl.multiple_of(cb * LANES, LANES)
                hit = slot_scr[e:e + 1, pl.ds(off, LANES)] == want
                idx_acc = idx_acc + jnp.where(hit, lane + off.astype(F32), 0.0)
                g_acc = g_acc + jnp.where(hit, aff_scr[e:e + 1, pl.ds(off, LANES)], 0.0)
                return idx_acc, g_acc

            zero = jnp.zeros((jb, LANES), F32)
            idx_acc, g_acc = lax.fori_loop(first, last + 1, per_chunk, (zero, zero))
            hit_scr[0, j0:j0 + jb, :] = idx_acc
            hit_scr[1, j0:j0 + jb, :] = g_acc
        idx_ref[0, e] = jnp.sum(hit_scr[0], axis=1, keepdims=True).astype(jnp.int32)
        gate_ref[0, e] = jnp.sum(hit_scr[1], axis=1, keepdims=True)


def _route(logits_t, cap):
    b, e, seq = logits_t.shape
    jb = min(cap, 64)
    return pl.pallas_call(
        functools.partial(_route_kernel, seq, cap, jb),
        grid=(b,),
        in_specs=[pl.BlockSpec((1, e, seq), lambda i: (i, 0, 0))],
        out_specs=[pl.BlockSpec((1, e, cap, 1), lambda i: (i, 0, 0, 0)),
                   pl.BlockSpec((1, e, cap, 1), lambda i: (i, 0, 0, 0))],
        out_shape=[jax.ShapeDtypeStruct((b, e, cap, 1), jnp.int32), jax.ShapeDtypeStruct((b, e, cap, 1), F32)],
        scratch_shapes=[pltpu.VMEM((e, seq), F32), pltpu.VMEM((e, seq), F32), pltpu.VMEM((2, cap, LANES), F32)],
        compiler_params=_params(("parallel",)), name="route",
    )(logits_t)


def _moe_up_kernel(cap, d_model, pitch, idx_ref, hg_ref, wg_ref, wu_ref, hid_ref, xg_scr, xs_scr):
    b, e = pl.program_id(0), pl.program_id(1)

    def gather(expert, slot, j):
        t = idx_ref[(b * N_EXPERTS + expert) * cap + j]
        src = pl.multiple_of(t * BF16_SUBLANES, BF16_SUBLANES)
        tile = hg_ref[0, pl.ds(src, BF16_SUBLANES), :].astype(F32)
        xg_scr[slot, pl.ds(j, SUBLANES, stride=pitch), :] = tile[:SUBLANES]
        xg_scr[slot, pl.ds(SUBLANES * pitch + j, SUBLANES, stride=pitch), :] = tile[SUBLANES:]

    @pl.when(e == 0)
    def _():
        def first(j, _):
            gather(0, 0, j)
            return 0
        lax.fori_loop(0, cap, first, 0, unroll=8)

    slot = e % 2
    for c in range(d_model // LANES):
        xs_scr[:, c * LANES:(c + 1) * LANES] = xg_scr[slot, c * pitch:c * pitch + cap, :].astype(BF16)
    xs = xs_scr[...]
    g = jnp.dot(xs, wg_ref[0, 0], preferred_element_type=F32)
    u = jnp.dot(xs, wu_ref[0, 0], preferred_element_type=F32)
    hid_ref[0, 0] = (g * _sigmoid(g) * u).astype(BF16)
    nxt = jnp.minimum(e + 1, N_EXPERTS - 1)
    for j in range(cap):
        gather(nxt, 1 - slot, j)


def _moe_up(idx_flat, hg, wg, wu, *, layer, cap):
    b, rows, _ = hg.shape
    _, n_e, d_model, d_ff = wg.shape
    pitch = cap + SUBLANES
    grid_spec = pltpu.PrefetchScalarGridSpec(
        num_scalar_prefetch=1, grid=(b, n_e),
        in_specs=[pl.BlockSpec((1, rows, LANES), lambda i, e, idx: (i, 0, 0), pipeline_mode=pl.Buffered(1)),
                  pl.BlockSpec((1, 1, d_model, d_ff), lambda i, e, idx: (layer, e, 0, 0)),
                  pl.BlockSpec((1, 1, d_model, d_ff), lambda i, e, idx: (layer, e, 0, 0))],
        out_specs=pl.BlockSpec((1, 1, cap, d_ff), lambda i, e, idx: (i, e, 0, 0)),
        scratch_shapes=[pltpu.VMEM((2, d_model // LANES * pitch, LANES), F32), pltpu.VMEM((cap, d_model), BF16)])
    return pl.pallas_call(
        functools.partial(_moe_up_kernel, cap, d_model, pitch),
        grid_spec=grid_spec,
        out_shape=jax.ShapeDtypeStruct((b, n_e, cap, d_ff), BF16),
        compiler_params=_params(("arbitrary", "arbitrary")), name="moe_up",
    )(idx_flat, hg, wg, wu)


def _moe_down_kernel(cap, half, pitch, group, idx_ref, hid_ref, gate_ref, wd_ref, acc_ref, og_scr):
    b, e = pl.program_id(0), pl.program_id(2)

    def project(slot):
        o = jnp.dot(hid_ref[0, 0], wd_ref[0, 0], preferred_element_type=F32) * gate_ref[0]
        for c in range(half // LANES):
            og_scr[slot, c * pitch:c * pitch + cap, :] = o[:, c * LANES:(c + 1) * LANES]

    def scatter(expert, slot, j0):
        base = (b * N_EXPERTS + expert) * cap
        dsts = [pl.multiple_of(idx_ref[base + j0 + k] * SUBLANES, SUBLANES) for k in range(group)]
        vals = [acc_ref[0, 0, pl.ds(dsts[k], SUBLANES), :] + og_scr[slot, pl.ds(j0 + k, SUBLANES, stride=pitch), :]
                for k in range(group)]
        for k in range(group):
            acc_ref[0, 0, pl.ds(dsts[k], SUBLANES), :] = vals[k]

    slot = e % 2

    @pl.when(e == 0)
    def _():
        acc_ref[...] = jnp.zeros_like(acc_ref)
        project(slot)

    @pl.when(e > 0)
    def _():
        for jg in range(cap // group):
            scatter(e - 1, 1 - slot, jg * group)
        project(slot)

    @pl.when(e == N_EXPERTS - 1)
    def _():
        def flush(jg, _):
            scatter(e, slot, jg * group)
            return 0
        lax.fori_loop(0, cap // group, flush, 0)


def _moe_down(idx_flat, hid, gate, wd, *, layer, seq):
    b, n_e, cap, d_ff = hid.shape
    d_model = wd.shape[3]
    half = d_model // 2
    pitch = cap + SUBLANES
    group = min(cap, 8)
    grid_spec = pltpu.PrefetchScalarGridSpec(
        num_scalar_prefetch=1, grid=(b, 2, n_e),
        in_specs=[pl.BlockSpec((1, 1, cap, d_ff), lambda i, p, e, idx: (i, e, 0, 0)),
                  pl.BlockSpec((1, cap, 1), lambda i, p, e, idx: (i * n_e + e, 0, 0)),
                  pl.BlockSpec((1, 1, d_ff, half), lambda i, p, e, idx: (layer, e, 0, p))],
        out_specs=pl.BlockSpec((1, 1, seq * SUBLANES, LANES), lambda i, p, e, idx: (i, p, 0, 0)),
        scratch_shapes=[pltpu.VMEM((2, half // LANES * pitch, LANES), F32)])
    return pl.pallas_call(
        functools.partial(_moe_down_kernel, cap, half, pitch, group),
        grid_spec=grid_spec,
        out_shape=jax.ShapeDtypeStruct((b, 2, seq * SUBLANES, LANES), F32),
        compiler_params=_params(("arbitrary", "arbitrary", "arbitrary")), name="moe_down",
    )(idx_flat, hid, gate, wd)


def _final_kernel(tm, x_ref, moe_ref, g2_ref, g_ref, y_ref):
    x = _moe_residual(x_ref[...], moe_ref, g2_ref[0], tm)
    y_ref[...] = _rms(x, g_ref[...])


def _final(x, moe, mod_prev, g, *, tm, tiles_per_mod):
    t, d = x.shape
    tiles_per_row = moe.shape[2] // (tm * SUBLANES)
    return pl.pallas_call(
        functools.partial(_final_kernel, tm),
        grid=(t // tm,),
        in_specs=[pl.BlockSpec((tm, d), lambda i: (i, 0)),
                  pl.BlockSpec((1, 2, tm * SUBLANES, LANES), lambda i: (i // tiles_per_row, 0, i % tiles_per_row, 0)),
                  pl.BlockSpec((1, 1, d), lambda i: (i // tiles_per_mod, 0, 5)),
                  pl.BlockSpec(g.shape, lambda i: (0, 0))],
        out_specs=pl.BlockSpec((tm, d), lambda i: (i, 0)),
        out_shape=jax.ShapeDtypeStruct((t, d), F32),
        compiler_params=_params(("parallel",)), name="final_norm",
    )(x, moe, mod_prev, g)


def _rope_swap_perm():
    q = QK_ROPE // 4
    return jnp.array(list(range(q, 2 * q)) + list(range(0, q)) + list(range(3 * q, 4 * q)) + list(range(2 * q, 3 * q)),
                     jnp.int32)


def _rope_table(n_tokens):
    rows = n_tokens // GRID_W
    row = jnp.repeat(jnp.arange(rows, dtype=F32), GRID_W)
    col = jnp.tile(jnp.arange(GRID_W, dtype=F32), rows)
    n_pairs = QK_ROPE // 4
    inv = ROPE_BASE ** (-jnp.arange(n_pairs, dtype=F32) / n_pairs)
    ang_r = row[:, None] * inv
    ang_c = col[:, None] * inv
    cr, sr, cc, sn = jnp.cos(ang_r), jnp.sin(ang_r), jnp.cos(ang_c), jnp.sin(ang_c)
    return jnp.concatenate([cr, cr, cc, cc, -sr, sr, -sn, sn], axis=1)


def _layer_weights(l, w):
    d = w["w_in"].shape[1]
    q_lora = w["q_norm_g"].shape[1]
    kv_lora = w["kv_norm_g"].shape[1]
    perm = _rope_swap_perm()
    w_in = w["w_in"][l]
    c1 = q_lora + kv_lora
    kr_w = w_in[:, c1:c1 + QK_ROPE]
    w_in_p = jnp.concatenate([w_in[:, :c1 + QK_ROPE], kr_w[:, perm], w_in[:, c1 + QK_ROPE:]], axis=1).astype(BF16)
    w_uq = w["w_uq"][l]
    rope_w = w_uq[:, :, QK_NOPE:]
    w_uq_p = jnp.concatenate([w_uq, rope_w[:, :, perm]], axis=2).reshape(q_lora, N_HEADS * HEAD_SLAB).astype(BF16)
    w_ukv = w["w_ukv"][l]
    w_uk_p = w_ukv[:, :, :QK_NOPE].reshape(kv_lora, -1).astype(BF16)
    w_vt_p = w_ukv[:, :, QK_NOPE:].reshape(kv_lora, -1).T.astype(BF16)
    return dict(
        w_in=w_in_p, q_norm_g=w["q_norm_g"][l][None], w_uq=w_uq_p, kv_norm_g=w["kv_norm_g"][l][None], w_uk=w_uk_p, w_vt=w_vt_p,
        norm1_g=w["norm1_g"][l][None], norm2_g=w["norm2_g"][l][None],
        sc_conv_w=w["sc_conv_w"][l], cf_conv_w=w["cf_conv_w"][l], cf_conv_b=w["cf_conv_b"][l][None],
        cf_ln_g=w["cf_ln_g"][l][None], cf_ln_b=w["cf_ln_b"][l][None],
        w_o=w["w_o"][l].astype(BF16), router_wt=w["router_w"][l].T,
        layer=l)


def _run_stream(x0, mods, lws, ew, final_g, *, batch, seq, tab, cache_k, cache_v, merge_moe_rows, tm, tq, tk, n_hd, ahead):
    t, d = x0.shape
    n_mod = mods[0].shape[0]
    tiles_per_mod = t // n_mod // tm
    tiles_per_seq = seq // tm
    cap = max(1, EC_CAPACITY * seq // N_EXPERTS)
    moe_b, moe_seq = (1, t) if merge_moe_rows else (batch, seq)
    moe_cap = cap * (batch // moe_b)
    x, moe = x0, None
    ckvs, krs = [], []
    for l, lw in enumerate(lws):
        x, q, k, v, cv, ckvn, kr = _inproj(x, moe, mods[l - 1] if l else None, mods[l], lw["norm1_g"], tab, lw,
                                           tm=tm, tiles_per_mod=tiles_per_mod, tiles_per_seq=tiles_per_seq)
        ckvs.append(ckvn)
        krs.append(kr)
        kc = None if cache_k is None else cache_k[l]
        vc = None if cache_v is None else cache_v[l]
        o_att = _attention(q, k, v, kc, vc, batch=batch, seq=seq, tq=tq, tk=tk, n_hd=n_hd, ahead=ahead)
        x, hg, logits_t = _mixout(o_att, cv, x, mods[l], lw, tm=tm, tiles_per_mod=tiles_per_mod,
                                  tiles_per_seq=tiles_per_seq, route_batch=batch, route_len=seq)
        idx, gate = _route(logits_t, cap)
        if moe_b != batch:
            offs = (jnp.arange(batch, dtype=jnp.int32) * seq)[:, None, None, None]
            idx = jnp.transpose(idx + offs, (1, 0, 2, 3)).reshape(1, N_EXPERTS, moe_cap, 1)
            gate = jnp.transpose(gate, (1, 0, 2, 3)).reshape(1, N_EXPERTS, moe_cap, 1)
        idx_flat = idx.reshape(-1)
        gate = gate.reshape(moe_b * N_EXPERTS, moe_cap, 1)
        hid = _moe_up(idx_flat, hg.reshape(moe_b, moe_seq * BF16_SUBLANES, LANES), ew["wg"], ew["wu"], layer=lw["layer"], cap=moe_cap)
        moe = _moe_down(idx_flat, hid, gate, ew["wd"], layer=lw["layer"], seq=moe_seq)
    y = _final(x, moe, mods[-1], final_g, tm=tm, tiles_per_mod=tiles_per_mod)
    return y, ckvs, krs


def kernel(x_prompt, x_sample, cache_ckv, cache_krope, c, c_ctx, ada_w, ada_b, norm1_g, norm2_g, w_in, q_norm_g, w_uq,
           kv_norm_g, w_ukv, sc_conv_w, cf_conv_w, cf_conv_b, cf_ln_g, cf_ln_b, w_o, router_w, exp_w_gate, exp_w_up,
           exp_w_down, final_norm_g):
    batch, seq, d = x_prompt.shape
    dec_batch, dec_seq, _ = x_sample.shape
    depth = ada_w.shape[0]
    past = cache_ckv.shape[2]
    w = dict(norm1_g=norm1_g, norm2_g=norm2_g, w_in=w_in, q_norm_g=q_norm_g, w_uq=w_uq, kv_norm_g=kv_norm_g, w_ukv=w_ukv,
             sc_conv_w=sc_conv_w, cf_conv_w=cf_conv_w, cf_conv_b=cf_conv_b, cf_ln_g=cf_ln_g, cf_ln_b=cf_ln_b, w_o=w_o,
             router_w=router_w, exp_w_gate=exp_w_gate, exp_w_up=exp_w_up, exp_w_down=exp_w_down)
    lws = [_layer_weights(l, w) for l in range(depth)]
    ew = dict(wg=exp_w_gate.astype(BF16), wu=exp_w_up.astype(BF16), wd=exp_w_down.astype(BF16))

    n_rows = -(-(1 + dec_batch) // SUBLANES) * SUBLANES
    c_all = jnp.zeros((n_rows, d), F32).at[0].set(c_ctx).at[1:1 + dec_batch].set(c)
    mod = _ada_mod(c_all, ada_w, ada_b)
    mods_p = [mod[l, 0:1][:, None, :] for l in range(depth)]
    mods_s = [mod[l, 1:1 + dec_batch][:, None, :] for l in range(depth)]

    tm = 256
    final_g = final_norm_g[None]

    ident = jnp.concatenate([jnp.ones((tm, QK_ROPE), F32), jnp.zeros((tm, QK_ROPE), F32)], axis=1)
    y_p, ckvs, krs = _run_stream(x_prompt.reshape(batch * seq, d), mods_p, lws, ew, final_g, batch=batch, seq=seq, tab=ident,
                                 cache_k=None, cache_v=None, merge_moe_rows=True, tm=tm, tq=min(seq, 256), tk=min(seq, 512), n_hd=N_HEADS, ahead=1)

    ckv_c = jnp.transpose(cache_ckv, (1, 0, 2, 3))
    kr_c = jnp.transpose(cache_krope, (1, 0, 2, 3))
    kr_c = jnp.pad(kr_c, ((0, 0), (0, 0), (0, 0), (0, LANES - kr_c.shape[-1])))
    ck, cvv = _cache_kv(ckv_c, kr_c, jnp.stack([lw["w_uk"] for lw in lws]), jnp.stack([lw["w_vt"] for lw in lws]))
    y_s, _, _ = _run_stream(x_sample.reshape(dec_batch * dec_seq, d), mods_s, lws, ew, final_g, batch=dec_batch, seq=dec_seq,
                            tab=_rope_table(dec_seq), cache_k=ck, cache_v=cvv, merge_moe_rows=False, tm=tm, tq=256, tk=min(512, dec_seq // 4), n_hd=4, ahead=3)

    new_ckv = jnp.stack([a.reshape(batch, seq, -1) for a in ckvs], axis=1)
    new_kr = jnp.stack([a.reshape(batch, seq, -1) for a in krs], axis=1)
    return (y_p.reshape(batch, seq, d), y_s.reshape(dec_batch, dec_seq, d), new_ckv, new_kr)
```

```python
import functools

import jax
import jax.numpy as jnp
from jax import lax
from jax.experimental import pallas as pl
from jax.experimental.pallas import tpu as pltpu

LANES = 128
SUBLANES = 8
BF16_SUBLANES = 16
VMEM_LIMIT_BYTES = 56 * 1024 * 1024

N_HEADS = 8
QK_NOPE = 128
QK_ROPE = 64
V_DIM = 128
HEAD_SLAB = 256
N_EXPERTS = 16
EC_CAPACITY = 2
GRID_W = 64
SC_WIDTH = 3
CF_WIDTH = 31
CF_PAD = (CF_WIDTH - 1) // 2
HALO = 16
CONV_ROWS = 64
HG_PITCH = 24
ROPE_BASE = 10000.0
NORM_EPS = 1e-6
ATTN_SCALE = (QK_NOPE + QK_ROPE) ** -0.5
LOG2_E = 1.4426950408889634
Q_SCALE = ATTN_SCALE * LOG2_E

F32 = jnp.float32
BF16 = jnp.bfloat16


def _params(sem):
    return pltpu.CompilerParams(dimension_semantics=sem, vmem_limit_bytes=VMEM_LIMIT_BYTES)


def _sigmoid(x):
    return 1.0 / (1.0 + jnp.exp(-x))


def _rms(x, g):
    return x * lax.rsqrt(jnp.mean(x * x, axis=-1, keepdims=True) + NORM_EPS) * g


def _split3_dot(a, b, dims):
    a_hi = a.astype(BF16)
    a_lo = (a - a_hi.astype(F32)).astype(BF16)
    b_hi = b.astype(BF16)
    b_lo = (b - b_hi.astype(F32)).astype(BF16)
    dot = functools.partial(lax.dot_general, dimension_numbers=(dims, ((), ())), preferred_element_type=F32)
    return dot(a_hi, b_hi) + dot(a_lo, b_hi) + dot(a_hi, b_lo)


def _ada_kernel(c_ref, w_ref, b_ref, o_ref):
    cv = c_ref[...]
    s = cv * _sigmoid(cv)
    o_ref[0] = _split3_dot(s, w_ref[0], ((1,), (0,))) + b_ref[0]


def _ada_mod(c_all, ada_w, ada_b):
    depth, d, n6 = ada_w.shape
    rows = c_all.shape[0]
    tn = 1024
    return pl.pallas_call(
        _ada_kernel,
        grid=(depth, n6 // tn),
        in_specs=[pl.BlockSpec((rows, d), lambda l, j: (0, 0)),
                  pl.BlockSpec((1, d, tn), lambda l, j: (l, 0, j)),
                  pl.BlockSpec((1, 1, tn), lambda l, j: (l, 0, j))],
        out_specs=pl.BlockSpec((1, rows, tn), lambda l, j: (l, 0, j)),
        out_shape=jax.ShapeDtypeStruct((depth, rows, n6), F32),
        compiler_params=_params(("parallel", "parallel")),
        name="ada_mod",
    )(c_all, ada_w, ada_b.reshape(depth, 1, n6))


def _moe_residual(x, moe_ref, g2, tm):
    chunks = []
    for p in range(2):
        for c in range(SUBLANES):
            chunks.append(moe_ref[0, p, pl.ds(c, tm, stride=SUBLANES), :])
    return x + g2 * jnp.concatenate(chunks, axis=1)


def _rope_pair(v, tab):
    prod = v * tab
    r = prod + pltpu.roll(prod, QK_ROPE, 1)
    lane = lax.broadcasted_iota(jnp.int32, r.shape, 1)
    return jnp.where(lane < QK_ROPE, r, 0.0)


def _inproj_kernel(has_moe, tm, d_model, q_lora, kv_lora, *refs):
    if has_moe:
        (x_ref, moe_ref, g2_ref, shsc_ref, n1_ref, tab_ref, win_ref, qg_ref, wuq_ref, kvg_ref, wuk_ref, wvt_ref,
         xo_ref, q_ref, k_ref, vt_ref, cv_ref, ckv_ref, kr_ref) = refs
        x = _moe_residual(x_ref[...], moe_ref, g2_ref[0], tm)
        xo_ref[...] = x
    else:
        (x_ref, shsc_ref, n1_ref, tab_ref, win_ref, qg_ref, wuq_ref, kvg_ref, wuk_ref, wvt_ref,
         q_ref, k_ref, vt_ref, cv_ref, ckv_ref, kr_ref) = refs
        x = x_ref[...]
    sh1 = shsc_ref[0, :, :d_model]
    sc1 = shsc_ref[0, :, d_model:]
    h = _rms(x, n1_ref[...]) * (1.0 + sc1) + sh1
    u = jnp.dot(h.astype(BF16), win_ref[...], preferred_element_type=F32)
    tab = tab_ref[...]

    cqn = _rms(u[:, :q_lora], qg_ref[...])
    qf = jnp.dot(cqn.astype(BF16), wuq_ref[...], preferred_element_type=F32)
    for hd in range(N_HEADS):
        o = hd * HEAD_SLAB
        q_ref[:, o:o + QK_NOPE] = (qf[:, o:o + QK_NOPE] * Q_SCALE).astype(BF16)
        q_ref[:, o + QK_NOPE:o + HEAD_SLAB] = (_rope_pair(qf[:, o + QK_NOPE:o + HEAD_SLAB], tab) * Q_SCALE).astype(BF16)

    c0 = q_lora
    ckvn = _rms(u[:, c0:c0 + kv_lora], kvg_ref[...])
    ckv_ref[...] = ckvn
    ckvb = ckvn.astype(BF16)
    knf = jnp.dot(ckvb, wuk_ref[...], preferred_element_type=F32)
    vt_ref[0] = lax.dot_general(wvt_ref[...], ckvb, (((1,), (1,)), ((), ())), preferred_element_type=F32).astype(BF16)
    c1 = c0 + kv_lora
    krv = u[:, c1:c1 + LANES]
    kr_ref[...] = krv[:, :QK_ROPE]
    kr2 = _rope_pair(krv, tab).astype(BF16)
    for hd in range(N_HEADS):
        o = hd * HEAD_SLAB
        k_ref[:, o:o + QK_NOPE] = knf[:, hd * QK_NOPE:(hd + 1) * QK_NOPE].astype(BF16)
        k_ref[:, o + QK_NOPE:o + HEAD_SLAB] = kr2
    cv_ref[...] = u[:, c1 + LANES:]


def _inproj(x, moe, mod_prev, mod, n1, tab, lw, *, tm, tiles_per_mod, tiles_per_seq):
    t, d = x.shape
    nt = t // tm
    p_in = lw["w_in"].shape[1]
    q_lora = lw["q_norm_g"].shape[1]
    kv_lora = lw["kv_norm_g"].shape[1]
    n_cv = p_in - q_lora - kv_lora - LANES
    has_moe = moe is not None
    tiles_per_row = None if moe is None else moe.shape[2] // (tm * SUBLANES)

    full = lambda a: pl.BlockSpec(a.shape, lambda i: (0,) * a.ndim)
    row = lambda w: pl.BlockSpec((tm, w), lambda i: (i, 0))
    in_specs = [row(d)]
    args = [x]
    if has_moe:
        in_specs += [pl.BlockSpec((1, 2, tm * SUBLANES, LANES), lambda i: (i // tiles_per_row, 0, i % tiles_per_row, 0)),
                     pl.BlockSpec((1, 1, d), lambda i: (i // tiles_per_mod, 0, 5))]
        args += [moe, mod_prev]
    in_specs += [pl.BlockSpec((1, 1, 2 * d), lambda i: (i // tiles_per_mod, 0, 0)),
                 full(n1),
                 pl.BlockSpec((tm, LANES), lambda i: (i % tiles_per_seq, 0)),
                 full(lw["w_in"]), full(lw["q_norm_g"]), full(lw["w_uq"]), full(lw["kv_norm_g"]), full(lw["w_uk"]), full(lw["w_vt"])]
    args += [mod, n1, tab, lw["w_in"], lw["q_norm_g"], lw["w_uq"], lw["kv_norm_g"], lw["w_uk"], lw["w_vt"]]

    out_specs, out_shape = [], []
    if has_moe:
        out_specs.append(row(d))
        out_shape.append(jax.ShapeDtypeStruct((t, d), F32))
    seq = tiles_per_seq * tm
    for w, dt in ((N_HEADS * HEAD_SLAB, BF16), (N_HEADS * HEAD_SLAB, BF16), (None, BF16),
                  (n_cv, F32), (kv_lora, F32), (QK_ROPE, F32)):
        if w is None:
            out_specs.append(pl.BlockSpec((1, N_HEADS * V_DIM, tm), lambda i: (i // tiles_per_seq, 0, i % tiles_per_seq)))
            out_shape.append(jax.ShapeDtypeStruct((t // seq, N_HEADS * V_DIM, seq), dt))
        else:
            out_specs.append(row(w))
            out_shape.append(jax.ShapeDtypeStruct((t, w), dt))

    outs = pl.pallas_call(
        functools.partial(_inproj_kernel, has_moe, tm, d, q_lora, kv_lora),
        grid=(nt,), in_specs=in_specs, out_specs=out_specs, out_shape=out_shape,
        compiler_params=_params(("parallel",)), name="inproj",
    )(*args)
    if not has_moe:
        outs = [x] + list(outs)
    return outs


def _cachekv_kernel(ckv_ref, kr_ref, wuk_ref, wvt_ref, k_ref, vt_ref):
    ckvb = ckv_ref[0, 0].astype(BF16)
    knf = jnp.dot(ckvb, wuk_ref[0], preferred_element_type=F32)
    kr2 = kr_ref[0, 0].astype(BF16)
    for hd in range(N_HEADS):
        o = hd * HEAD_SLAB
        k_ref[0, 0, :, o:o + QK_NOPE] = knf[:, hd * QK_NOPE:(hd + 1) * QK_NOPE].astype(BF16)
        k_ref[0, 0, :, o + QK_NOPE:o + HEAD_SLAB] = kr2
    vt_ref[0, 0] = lax.dot_general(wvt_ref[0], ckvb, (((1,), (1,)), ((), ())), preferred_element_type=F32).astype(BF16)


def _cache_kv(ckv_c, kr_c, w_uk_all, w_vt_all):
    depth, b, past, kv_lora = ckv_c.shape
    nk = N_HEADS * HEAD_SLAB
    nv = N_HEADS * V_DIM
    return pl.pallas_call(
        _cachekv_kernel,
        grid=(depth, b),
        in_specs=[pl.BlockSpec((1, 1, past, kv_lora), lambda l, i: (l, i, 0, 0)),
                  pl.BlockSpec((1, 1, past, LANES), lambda l, i: (l, i, 0, 0)),
                  pl.BlockSpec((1, kv_lora, N_HEADS * QK_NOPE), lambda l, i: (l, 0, 0)),
                  pl.BlockSpec((1, nv, kv_lora), lambda l, i: (l, 0, 0))],
        out_specs=[pl.BlockSpec((1, 1, past, nk), lambda l, i: (l, i, 0, 0)),
                   pl.BlockSpec((1, 1, nv, past), lambda l, i: (l, i, 0, 0))],
        out_shape=[jax.ShapeDtypeStruct((depth, b, past, nk), BF16), jax.ShapeDtypeStruct((depth, b, nv, past), BF16)],
        compiler_params=_params(("parallel", "parallel")), name="cache_kv",
    )(ckv_c, kr_c, w_uk_all, w_vt_all)


def _attn_kernel(has_ctx, n_hd, ahead, lk, tk, *refs):
    if has_ctx:
        q_ref, k_ref, vt_ref, kc_ref, vct_ref, o_ref, s_scr, sc_scr = refs
    else:
        q_ref, k_ref, vt_ref, o_ref, s_scr = refs
    tq = q_ref.shape[0]
    n = lk // tk
    qs = [q_ref[:, g * HEAD_SLAB:(g + 1) * HEAD_SLAB] for g in range(n_hd)]

    def scores(g, kc, s_ref):
        s = lax.dot_general(kc, qs[g], (((1,), (1,)), ((), ())), preferred_element_type=F32)
        s_ref[...] = s
        return jnp.max(s, axis=0, keepdims=True)

    def k_chunk(g, c):
        return k_ref[c * tk:(c + 1) * tk, g * HEAD_SLAB:(g + 1) * HEAD_SLAB]

    def vt_chunk(g, c):
        return vt_ref[0, g * V_DIM:(g + 1) * V_DIM, c * tk:(c + 1) * tk]

    def accumulate(state, s_ref, smax, vt):
        m, l, acc = state
        m_new = jnp.maximum(m, smax)
        alpha = jnp.exp2(m - m_new)
        p = jnp.exp2(s_ref[...] - m_new)
        l = alpha * l + jnp.sum(p, axis=0, keepdims=True)
        acc = alpha * acc + jnp.dot(vt, p.astype(BF16), preferred_element_type=F32)
        return m_new, l, acc

    total = n + (1 if has_ctx else 0)
    n_slots = s_scr.shape[1]

    def s_of(g, c):
        return s_scr.at[g, c % n_slots] if c < n else sc_scr.at[g]

    def issue(g, c):
        kc = k_chunk(g, c) if c < n else kc_ref[0, :, g * HEAD_SLAB:(g + 1) * HEAD_SLAB]
        return scores(g, kc, s_of(g, c))

    def vt_of(g, c):
        return vt_chunk(g, c) if c < n else vct_ref[0, g * V_DIM:(g + 1) * V_DIM, :]

    states = [(jnp.full((1, tq), -jnp.inf, F32), jnp.zeros((1, tq), F32), jnp.zeros((V_DIM, tq), F32))] * n_hd
    smax = {}
    for c in range(min(ahead, total)):
        for g in range(n_hd):
            smax[g, c] = issue(g, c)
    for c in range(total):
        for g in range(n_hd):
            if c + ahead < total:
                smax[g, c + ahead] = issue(g, c + ahead)
            states[g] = accumulate(states[g], s_of(g, c), smax.pop((g, c)), vt_of(g, c))
    for g in range(n_hd):
        _, l, acc = states[g]
        o_ref[:, g * V_DIM:(g + 1) * V_DIM] = jnp.transpose(acc / l).astype(BF16)


def _attention(q, k, vt, kc, vct, *, batch, seq, tq, tk, n_hd, ahead):
    t = q.shape[0]
    nq = seq // tq
    has_ctx = kc is not None
    in_specs = [pl.BlockSpec((tq, n_hd * HEAD_SLAB), lambda b, h, i: (b * nq + i, h)),
                pl.BlockSpec((seq, n_hd * HEAD_SLAB), lambda b, h, i: (b, h)),
                pl.BlockSpec((1, n_hd * V_DIM, seq), lambda b, h, i: (b, h, 0))]
    args = [q, k, vt]
    scratch = [pltpu.VMEM((n_hd, ahead + 1, tk, tq), F32)]
    if has_ctx:
        past = kc.shape[1]
        in_specs += [pl.BlockSpec((1, past, n_hd * HEAD_SLAB), lambda b, h, i: (b, 0, h)),
                     pl.BlockSpec((1, n_hd * V_DIM, past), lambda b, h, i: (b, h, 0))]
        args += [kc, vct]
        scratch += [pltpu.VMEM((n_hd, past, tq), F32)]
    return pl.pallas_call(
        functools.partial(_attn_kernel, has_ctx, n_hd, ahead, seq, tk),
        grid=(batch, N_HEADS // n_hd, nq), in_specs=in_specs,
        out_specs=pl.BlockSpec((tq, n_hd * V_DIM), lambda b, h, i: (b * nq + i, h)),
        out_shape=jax.ShapeDtypeStruct((t, N_HEADS * V_DIM), BF16),
        scratch_shapes=scratch,
        compiler_params=_params(("parallel", "parallel", "arbitrary")), name="attention",
    )(*args)


def _mixout_kernel(tm, tiles_per_seq, d_sc, d_cf,
                   o_ref, cv_ref, cvp_ref, cvn_ref, x_ref, g1_ref, sh2_ref, sc2_ref,
                   scw_ref, cfw_ref, cfb_ref, lng_ref, lnb_ref, wo_ref, n2_ref, rw_ref,
                   xo_ref, hg_ref, lg_ref, ext_sc, ext_cf, ph_scr, z_scr, hg_scr):
    i = pl.program_id(0)
    first = (i % tiles_per_seq) == 0
    last = (i % tiles_per_seq) == tiles_per_seq - 1
    o_scx, o_scb, o_scc, o_cfa, o_cfg = 0, d_sc, 2 * d_sc, 3 * d_sc, 3 * d_sc + d_cf

    @pl.when(i == 0)
    def _():
        hg_scr[...] = jnp.zeros_like(hg_scr)

    def sc_in(ref):
        return ref[:, o_scc:o_scc + d_sc] * ref[:, o_scx:o_scx + d_sc]

    def cf_in(ref):
        return ref[:, o_cfa:o_cfa + d_cf] * _sigmoid(ref[:, o_cfg:o_cfg + d_cf])

    ext_sc[0:HALO, :] = jnp.where(first, 0.0, sc_in(cvp_ref))
    ext_sc[HALO:HALO + tm, :] = sc_in(cv_ref)
    ext_sc[HALO + tm:, :] = jnp.where(last, 0.0, sc_in(cvn_ref))
    ext_cf[0:HALO, :] = jnp.where(first, 0.0, cf_in(cvp_ref))
    ext_cf[HALO:HALO + tm, :] = cf_in(cv_ref)
    ext_cf[HALO + tm:, :] = jnp.where(last, 0.0, cf_in(cvn_ref))

    sc_pad = (SC_WIDTH - 1) // 2
    y_sc = jnp.zeros((tm, d_sc), F32)
    for kk in range(SC_WIDTH):
        y_sc = y_sc + scw_ref[kk:kk + 1, :] * ext_sc[pl.ds(HALO - sc_pad + kk, tm), :]
    y_sc = cv_ref[:, o_scb:o_scb + d_sc] * y_sc

    first_row = HALO - CF_PAD
    span = tm + 2 * HALO - SUBLANES
    for phase in range(1, SUBLANES):
        ph_scr[phase - 1, 0:span, :] = ext_cf[pl.ds(phase, span), :]
    for ct in range(d_cf // LANES):
        cs = slice(ct * LANES, (ct + 1) * LANES)
        for rb in range(tm // CONV_ROWS):
            acc = jnp.zeros((CONV_ROWS, LANES), F32)
            for kk in range(CF_WIDTH):
                phase = (first_row + kk) % SUBLANES
                base = first_row + kk - phase + rb * CONV_ROWS
                src = ext_cf if phase == 0 else ph_scr.at[phase - 1]
                acc = acc + cfw_ref[kk:kk + 1, cs] * src[base:base + CONV_ROWS, cs]
            z_scr[rb * CONV_ROWS:(rb + 1) * CONV_ROWS, cs] = acc
    z = z_scr[...] + cfb_ref[...]
    mu = jnp.mean(z, axis=-1, keepdims=True)
    zc = z - mu
    var = jnp.mean(zc * zc, axis=-1, keepdims=True)
    z = zc * lax.rsqrt(var + NORM_EPS) * lng_ref[...] + lnb_ref[...]
    z = z * _sigmoid(z)

    mixed = jnp.concatenate([o_ref[...], y_sc.astype(BF16), z.astype(BF16)], axis=1)
    y = jnp.dot(mixed, wo_ref[...], preferred_element_type=F32)
    x = x_ref[...] + g1_ref[0] * y
    xo_ref[...] = x
    h2 = _rms(x, n2_ref[...]) * (1.0 + sc2_ref[0]) + sh2_ref[0]

    lg = _split3_dot(h2, rw_ref[...], ((1,), (0,)))
    lg_ref[0] = jnp.transpose(lg)[:N_EXPERTS]

    d_model = h2.shape[1]
    for c in range(d_model // LANES):
        hg_scr[pl.ds(c, tm, stride=HG_PITCH), :] = h2[:, c * LANES:(c + 1) * LANES]
    tiles = hg_scr[...].reshape(tm, HG_PITCH, LANES)[:, :BF16_SUBLANES, :]
    hg_ref[...] = tiles.reshape(tm * BF16_SUBLANES, LANES).astype(BF16)


def _mixout(o_att, cv, x, mod, lw, *, tm, tiles_per_mod, tiles_per_seq, route_batch, route_len):
    t, d = x.shape
    nt = t // tm
    n_cv = cv.shape[1]
    d_sc = lw["sc_conv_w"].shape[1]
    d_cf = lw["cf_conv_w"].shape[1]
    hb = tm // HALO
    n_halo = t // HALO
    tiles_per_route = route_len // tm
    full = lambda a: pl.BlockSpec(a.shape, lambda i: (0,) * a.ndim)
    row = lambda w: pl.BlockSpec((tm, w), lambda i: (i, 0))
    modspec = lambda blk: pl.BlockSpec((1, 1, d), lambda i: (i // tiles_per_mod, 0, blk))
    in_specs = [row(o_att.shape[1]), row(n_cv),
                pl.BlockSpec((HALO, n_cv), lambda i: (jnp.maximum(i * hb - 1, 0), 0)),
                pl.BlockSpec((HALO, n_cv), lambda i: (jnp.minimum((i + 1) * hb, n_halo - 1), 0)),
                row(d), modspec(2), modspec(3), modspec(4),
                full(lw["sc_conv_w"]), full(lw["cf_conv_w"]), full(lw["cf_conv_b"]), full(lw["cf_ln_g"]),
                full(lw["cf_ln_b"]), full(lw["w_o"]), full(lw["norm2_g"]), full(lw["router_w"])]
    args = [o_att, cv, cv, cv, x, mod, mod, mod, lw["sc_conv_w"], lw["cf_conv_w"], lw["cf_conv_b"], lw["cf_ln_g"],
            lw["cf_ln_b"], lw["w_o"], lw["norm2_g"], lw["router_w"]]
    out_specs = [row(d),
                 pl.BlockSpec((tm * BF16_SUBLANES, LANES), lambda i: (i, 0)),
                 pl.BlockSpec((1, N_EXPERTS, tm), lambda i: (i // tiles_per_route, 0, i % tiles_per_route))]
    out_shape = [jax.ShapeDtypeStruct((t, d), F32),
                 jax.ShapeDtypeStruct((t * BF16_SUBLANES, LANES), BF16),
                 jax.ShapeDtypeStruct((route_batch, N_EXPERTS, route_len), F32)]
    return pl.pallas_call(
        functools.partial(_mixout_kernel, tm, tiles_per_seq, d_sc, d_cf),
        grid=(nt,), in_specs=in_specs, out_specs=out_specs, out_shape=out_shape,
        scratch_shapes=[pltpu.VMEM((tm + 2 * HALO, d_sc), F32), pltpu.VMEM((tm + 2 * HALO, d_cf), F32),
                        pltpu.VMEM((SUBLANES - 1, tm + 2 * HALO, d_cf), F32), pltpu.VMEM((tm, d_cf), F32),
                        pltpu.VMEM((tm * HG_PITCH, LANES), F32)],
        compiler_params=_params(("arbitrary",)), name="mixout",
    )(*args)


def _route_kernel(seq, cap, jb, lg_ref, idx_ref, gate_ref, slot_scr, aff_scr, hit_scr):
    lg = lg_ref[0]
    ex = jnp.exp(lg - jnp.max(lg, axis=0, keepdims=True))
    aff = ex / jnp.sum(ex, axis=0, keepdims=True)
    bits = pltpu.bitcast(aff, jnp.int32)
    tok = lax.broadcasted_iota(jnp.int32, aff.shape, 1)
    capf = jnp.float32(cap)

    def count(mask):
        return jnp.sum(jnp.where(mask, 1.0, 0.0), axis=1, keepdims=True)

    def greedy_bits(n_bits, keep):
        v = jnp.zeros((N_EXPERTS, 1), jnp.int32)
        bit = n_bits - 1
        while bit >= 0:
            hi = jnp.int32(1 << bit)
            if bit == 0:
                v = jnp.where(keep(v | hi), v | hi, v)
                bit -= 1
            else:
                lo = jnp.int32(1 << (bit - 1))
                k11, k10, k01 = keep(v | hi | lo), keep(v | hi), keep(v | lo)
                v = jnp.where(k11, v | hi | lo, jnp.where(k10, v | hi, jnp.where(k01, v | lo, v)))
                bit -= 2
        return v

    thr = greedy_bits(31, lambda cand: count(bits >= cand) >= capf)
    gt = bits > thr
    eq = bits == thr
    need = capf - count(gt)
    cut = greedy_bits(max(seq.bit_length(), 1), lambda cand: count(eq & (tok < cand)) <= need)
    sel = gt | (eq & (tok < cut))

    r_i = lax.broadcasted_iota(jnp.int32, (LANES, LANES), 0)
    c_i = lax.broadcasted_iota(jnp.int32, (LANES, LANES), 1)
    tri = jnp.where(r_i < c_i, 1.0, 0.0).astype(BF16)
    self = jnp.where(sel, 1.0, 0.0)
    carry = jnp.zeros((N_EXPERTS, 1), F32)
    slots = []
    for blk in range(seq // LANES):
        seg = self[:, blk * LANES:(blk + 1) * LANES]
        excl = jnp.dot(seg.astype(BF16), tri, preferred_element_type=F32)
        slots.append(jnp.where(seg > 0.0, excl + carry, -1.0))
        carry = carry + jnp.sum(seg, axis=1, keepdims=True)
    slot_scr[...] = jnp.concatenate(slots, axis=1)
    aff_scr[...] = aff

    n_chunks = seq // LANES
    jcol = lax.broadcasted_iota(jnp.int32, (jb, LANES), 0).astype(F32)
    lane = lax.broadcasted_iota(jnp.int32, (jb, LANES), 1).astype(F32)

    n_blk = cap // jb
    lead_want = lax.broadcasted_iota(jnp.int32, (n_blk, LANES), 0).astype(F32) * float(jb)
    lead_lane = lax.broadcasted_iota(jnp.int32, (n_blk, LANES), 1).astype(F32)

    for e in range(N_EXPERTS):
        def lead_chunk(cb, acc, e=e):
            off = pl.multiple_of(cb * LANES, LANES)
            hit = slot_scr[e:e + 1, pl.ds(off, LANES)] == lead_want
            return acc + jnp.where(hit, lead_lane + off.astype(F32), 0.0)

        lead = lax.fori_loop(0, n_chunks, lead_chunk, jnp.zeros((n_blk, LANES), F32), unroll=min(n_chunks, 4))
        lead_tok = jnp.sum(lead, axis=1, keepdims=True).astype(jnp.int32)

        for blk in range(n_blk):
            j0 = blk * jb
            want = jcol + float(j0)
            first = lead_tok[blk, 0] // LANES
            last = lead_tok[blk + 1, 0] // LANES if blk + 1 < n_blk else n_chunks - 1

            def per_chunk(cb, carry, e=e, want=want):
                idx_acc, g_acc = carry
                off = pl.multiple_of(cb * LANES, LANES)
                hit = slot_scr[e:e + 1, pl.ds(off, LANES)] == want
                idx_acc = idx_acc + jnp.where(hit, lane + off.astype(F32), 0.0)
                g_acc = g_acc + jnp.where(hit, aff_scr[e:e + 1, pl.ds(off, LANES)], 0.0)
                return idx_acc, g_acc

            zero = jnp.zeros((jb, LANES), F32)
            idx_acc, g_acc = lax.fori_loop(first, last + 1, per_chunk, (zero, zero))
            hit_scr[0, j0:j0 + jb, :] = idx_acc
            hit_scr[1, j0:j0 + jb, :] = g_acc
        idx_ref[0, e] = jnp.sum(hit_scr[0], axis=1, keepdims=True).astype(jnp.int32)
        gate_ref[0, e] = jnp.sum(hit_scr[1], axis=1, keepdims=True)


def _route(logits_t, cap):
    b, e, seq = logits_t.shape
    jb = min(cap, 64)
    return pl.pallas_call(
        functools.partial(_route_kernel, seq, cap, jb),
        grid=(b,),
        in_specs=[pl.BlockSpec((1, e, seq), lambda i: (i, 0, 0))],
        out_specs=[pl.BlockSpec((1, e, cap, 1), lambda i: (i, 0, 0, 0)),
                   pl.BlockSpec((1, e, cap, 1), lambda i: (i, 0, 0, 0))],
        out_shape=[jax.ShapeDtypeStruct((b, e, cap, 1), jnp.int32), jax.ShapeDtypeStruct((b, e, cap, 1), F32)],
        scratch_shapes=[pltpu.VMEM((e, seq), F32), pltpu.VMEM((e, seq), F32), pltpu.VMEM((2, cap, LANES), F32)],
        compiler_params=_params(("parallel",)), name="route",
    )(logits_t)


def _moe_up_kernel(cap, d_model, pitch, idx_ref, hg_ref, wg_ref, wu_ref, hid_ref, xg_scr, xs_scr):
    b, e = pl.program_id(0), pl.program_id(1)

    def gather(expert, slot, j):
        src = pl.multiple_of(idx_ref[(b * N_EXPERTS + expert) * cap + j], BF16_SUBLANES)
        tile = hg_ref[0, pl.ds(src, BF16_SUBLANES), :].astype(F32)
        xg_scr[slot, pl.ds(j, SUBLANES, stride=pitch), :] = tile[:SUBLANES]
        xg_scr[slot, pl.ds(SUBLANES * pitch + j, SUBLANES, stride=pitch), :] = tile[SUBLANES:]

    @pl.when(e == 0)
    def _():
        def first(j, _):
            gather(0, 0, j)
            return 0
        lax.fori_loop(0, cap, first, 0, unroll=8)

    slot = e % 2
    for c in range(d_model // LANES):
        xs_scr[:, c * LANES:(c + 1) * LANES] = xg_scr[slot, c * pitch:c * pitch + cap, :].astype(BF16)
    xs = xs_scr[...]
    g = jnp.dot(xs, wg_ref[0, 0], preferred_element_type=F32)
    u = jnp.dot(xs, wu_ref[0, 0], preferred_element_type=F32)
    hid_ref[0, 0] = (g * _sigmoid(g) * u).astype(BF16)
    nxt = jnp.minimum(e + 1, N_EXPERTS - 1)
    for j in range(cap):
        gather(nxt, 1 - slot, j)


def _moe_up(idx_flat, hg, wg, wu, *, layer, cap):
    b, rows, _ = hg.shape
    _, n_e, d_model, d_ff = wg.shape
    pitch = cap + SUBLANES
    grid_spec = pltpu.PrefetchScalarGridSpec(
        num_scalar_prefetch=1, grid=(b, n_e),
        in_specs=[pl.BlockSpec((1, rows, LANES), lambda i, e, idx: (i, 0, 0), pipeline_mode=pl.Buffered(1)),
                  pl.BlockSpec((1, 1, d_model, d_ff), lambda i, e, idx: (layer, e, 0, 0)),
                  pl.BlockSpec((1, 1, d_model, d_ff), lambda i, e, idx: (layer, e, 0, 0))],
        out_specs=pl.BlockSpec((1, 1, cap, d_ff), lambda i, e, idx: (i, e, 0, 0)),
        scratch_shapes=[pltpu.VMEM((2, d_model // LANES * pitch, LANES), F32), pltpu.VMEM((cap, d_model), BF16)])
    return pl.pallas_call(
        functools.partial(_moe_up_kernel, cap, d_model, pitch),
        grid_spec=grid_spec,
        out_shape=jax.ShapeDtypeStruct((b, n_e, cap, d_ff), BF16),
        compiler_params=_params(("arbitrary", "arbitrary")), name="moe_up",
    )(idx_flat, hg, wg, wu)


def _moe_down_kernel(cap, half, pitch, group, idx_ref, hid_ref, gate_ref, wd_ref, acc_ref, og_scr):
    b, e = pl.program_id(0), pl.program_id(2)

    def project(slot):
        o = jnp.dot(hid_ref[0, 0], wd_ref[0, 0], preferred_element_type=F32) * gate_ref[0]
        for c in range(half // LANES):
            og_scr[slot, c * pitch:c * pitch + cap, :] = o[:, c * LANES:(c + 1) * LANES]

    def scatter(expert, slot, j0):
        base = (b * N_EXPERTS + expert) * cap
        dsts = [pl.multiple_of(idx_ref[base + j0 + k], SUBLANES) for k in range(group)]
        vals = [acc_ref[0, 0, pl.ds(dsts[k], SUBLANES), :] + og_scr[slot, pl.ds(j0 + k, SUBLANES, stride=pitch), :]
                for k in range(group)]
        for k in range(group):
            acc_ref[0, 0, pl.ds(dsts[k], SUBLANES), :] = vals[k]

    slot = e % 2

    @pl.when(e == 0)
    def _():
        acc_ref[...] = jnp.zeros_like(acc_ref)
        project(slot)

    @pl.when(e > 0)
    def _():
        for jg in range(cap // group):
            scatter(e - 1, 1 - slot, jg * group)
        project(slot)

    @pl.when(e == N_EXPERTS - 1)
    def _():
        def flush(jg, _):
            scatter(e, slot, jg * group)
            return 0
        lax.fori_loop(0, cap // group, flush, 0)


def _moe_down(idx_flat, hid, gate, wd, *, layer, seq):
    b, n_e, cap, d_ff = hid.shape
    d_model = wd.shape[3]
    half = d_model // 2
    pitch = cap + SUBLANES
    group = min(cap, 8)
    grid_spec = pltpu.PrefetchScalarGridSpec(
        num_scalar_prefetch=1, grid=(b, 2, n_e),
        in_specs=[pl.BlockSpec((1, 1, cap, d_ff), lambda i, p, e, idx: (i, e, 0, 0)),
                  pl.BlockSpec((1, cap, 1), lambda i, p, e, idx: (i * n_e + e, 0, 0)),
                  pl.BlockSpec((1, 1, d_ff, half), lambda i, p, e, idx: (layer, e, 0, p))],
        out_specs=pl.BlockSpec((1, 1, seq * SUBLANES, LANES), lambda i, p, e, idx: (i, p, 0, 0)),
        scratch_shapes=[pltpu.VMEM((2, half // LANES * pitch, LANES), F32)])
    return pl.pallas_call(
        functools.partial(_moe_down_kernel, cap, half, pitch, group),
        grid_spec=grid_spec,
        out_shape=jax.ShapeDtypeStruct((b, 2, seq * SUBLANES, LANES), F32),
        compiler_params=_params(("arbitrary", "arbitrary", "arbitrary")), name="moe_down",
    )(idx_flat, hid, gate, wd)


def _final_kernel(tm, x_ref, moe_ref, g2_ref, g_ref, y_ref):
    x = _moe_residual(x_ref[...], moe_ref, g2_ref[0], tm)
    y_ref[...] = _rms(x, g_ref[...])


def _final(x, moe, mod_prev, g, *, tm, tiles_per_mod):
    t, d = x.shape
    tiles_per_row = moe.shape[2] // (tm * SUBLANES)
    return pl.pallas_call(
        functools.partial(_final_kernel, tm),
        grid=(t // tm,),
        in_specs=[pl.BlockSpec((tm, d), lambda i: (i, 0)),
                  pl.BlockSpec((1, 2, tm * SUBLANES, LANES), lambda i: (i // tiles_per_row, 0, i % tiles_per_row, 0)),
                  pl.BlockSpec((1, 1, d), lambda i: (i // tiles_per_mod, 0, 5)),
                  pl.BlockSpec(g.shape, lambda i: (0, 0))],
        out_specs=pl.BlockSpec((tm, d), lambda i: (i, 0)),
        out_shape=jax.ShapeDtypeStruct((t, d), F32),
        compiler_params=_params(("parallel",)), name="final_norm",
    )(x, moe, mod_prev, g)


def _rope_swap_perm():
    q = QK_ROPE // 4
    return jnp.array(list(range(q, 2 * q)) + list(range(0, q)) + list(range(3 * q, 4 * q)) + list(range(2 * q, 3 * q)),
                     jnp.int32)


def _rope_table(n_tokens):
    rows = n_tokens // GRID_W
    row = jnp.repeat(jnp.arange(rows, dtype=F32), GRID_W)
    col = jnp.tile(jnp.arange(GRID_W, dtype=F32), rows)
    n_pairs = QK_ROPE // 4
    inv = ROPE_BASE ** (-jnp.arange(n_pairs, dtype=F32) / n_pairs)
    ang_r = row[:, None] * inv
    ang_c = col[:, None] * inv
    cr, sr, cc, sn = jnp.cos(ang_r), jnp.sin(ang_r), jnp.cos(ang_c), jnp.sin(ang_c)
    return jnp.concatenate([cr, cr, cc, cc, -sr, sr, -sn, sn], axis=1)


def _layer_weights(l, w):
    d = w["w_in"].shape[1]
    q_lora = w["q_norm_g"].shape[1]
    kv_lora = w["kv_norm_g"].shape[1]
    perm = _rope_swap_perm()
    w_in = w["w_in"][l]
    c1 = q_lora + kv_lora
    kr_w = w_in[:, c1:c1 + QK_ROPE]
    w_in_p = jnp.concatenate([w_in[:, :c1 + QK_ROPE], kr_w[:, perm], w_in[:, c1 + QK_ROPE:]], axis=1).astype(BF16)
    w_uq = w["w_uq"][l]
    rope_w = w_uq[:, :, QK_NOPE:]
    w_uq_p = jnp.concatenate([w_uq, rope_w[:, :, perm]], axis=2).reshape(q_lora, N_HEADS * HEAD_SLAB).astype(BF16)
    w_ukv = w["w_ukv"][l]
    w_uk_p = w_ukv[:, :, :QK_NOPE].reshape(kv_lora, -1).astype(BF16)
    w_vt_p = w_ukv[:, :, QK_NOPE:].reshape(kv_lora, -1).T.astype(BF16)
    return dict(
        w_in=w_in_p, q_norm_g=w["q_norm_g"][l][None], w_uq=w_uq_p, kv_norm_g=w["kv_norm_g"][l][None], w_uk=w_uk_p, w_vt=w_vt_p,
        norm1_g=w["norm1_g"][l][None], norm2_g=w["norm2_g"][l][None],
        sc_conv_w=w["sc_conv_w"][l], cf_conv_w=w["cf_conv_w"][l], cf_conv_b=w["cf_conv_b"][l][None],
        cf_ln_g=w["cf_ln_g"][l][None], cf_ln_b=w["cf_ln_b"][l][None],
        w_o=w["w_o"][l].astype(BF16), router_w=jnp.pad(w["router_w"][l], ((0, 0), (0, LANES - N_EXPERTS))),
        layer=l)


def _run_stream(x0, mods, lws, ew, final_g, *, batch, seq, tab, cache_k, cache_v, merge_moe_rows, tm, tq, tk, n_hd, ahead):
    t, d = x0.shape
    n_mod = mods[0].shape[0]
    tiles_per_mod = t // n_mod // tm
    tiles_per_seq = seq // tm
    cap = max(1, EC_CAPACITY * seq // N_EXPERTS)
    moe_b, moe_seq = (1, t) if merge_moe_rows else (batch, seq)
    moe_cap = cap * (batch // moe_b)
    x, moe = x0, None
    ckvs, krs = [], []
    for l, lw in enumerate(lws):
        x, q, k, v, cv, ckvn, kr = _inproj(x, moe, mods[l - 1] if l else None, mods[l], lw["norm1_g"], tab, lw,
                                           tm=tm, tiles_per_mod=tiles_per_mod, tiles_per_seq=tiles_per_seq)
        ckvs.append(ckvn)
        krs.append(kr)
        kc = None if cache_k is None else cache_k[l]
        vc = None if cache_v is None else cache_v[l]
        o_att = _attention(q, k, v, kc, vc, batch=batch, seq=seq, tq=tq, tk=tk, n_hd=n_hd, ahead=ahead)
        x, hg, logits_t = _mixout(o_att, cv, x, mods[l], lw, tm=tm, tiles_per_mod=tiles_per_mod,
                                  tiles_per_seq=tiles_per_seq, route_batch=batch, route_len=seq)
        idx, gate = _route(logits_t, cap)
        if moe_b != batch:
            offs = (jnp.arange(batch, dtype=jnp.int32) * seq)[:, None, None, None]
            idx = jnp.transpose(idx + offs, (1, 0, 2, 3)).reshape(1, N_EXPERTS, moe_cap, 1)
            gate = jnp.transpose(gate, (1, 0, 2, 3)).reshape(1, N_EXPERTS, moe_cap, 1)
        idx_flat = idx.reshape(-1)
        gate = gate.reshape(moe_b * N_EXPERTS, moe_cap, 1)
        hid = _moe_up(idx_flat * BF16_SUBLANES, hg.reshape(moe_b, moe_seq * BF16_SUBLANES, LANES), ew["wg"], ew["wu"], layer=lw["layer"], cap=moe_cap)
        moe = _moe_down(idx_flat * SUBLANES, hid, gate, ew["wd"], layer=lw["layer"], seq=moe_seq)
    y = _final(x, moe, mods[-1], final_g, tm=tm, tiles_per_mod=tiles_per_mod)
    return y, ckvs, krs


def kernel(x_prompt, x_sample, cache_ckv, cache_krope, c, c_ctx, ada_w, ada_b, norm1_g, norm2_g, w_in, q_norm_g, w_uq,
           kv_norm_g, w_ukv, sc_conv_w, cf_conv_w, cf_conv_b, cf_ln_g, cf_ln_b, w_o, router_w, exp_w_gate, exp_w_up,
           exp_w_down, final_norm_g):
    batch, seq, d = x_prompt.shape
    dec_batch, dec_seq, _ = x_sample.shape
    depth = ada_w.shape[0]
    past = cache_ckv.shape[2]
    w = dict(norm1_g=norm1_g, norm2_g=norm2_g, w_in=w_in, q_norm_g=q_norm_g, w_uq=w_uq, kv_norm_g=kv_norm_g, w_ukv=w_ukv,
             sc_conv_w=sc_conv_w, cf_conv_w=cf_conv_w, cf_conv_b=cf_conv_b, cf_ln_g=cf_ln_g, cf_ln_b=cf_ln_b, w_o=w_o,
             router_w=router_w, exp_w_gate=exp_w_gate, exp_w_up=exp_w_up, exp_w_down=exp_w_down)
    lws = [_layer_weights(l, w) for l in range(depth)]
    ew = dict(wg=exp_w_gate.astype(BF16), wu=exp_w_up.astype(BF16), wd=exp_w_down.astype(BF16))

    n_rows = -(-(1 + dec_batch) // SUBLANES) * SUBLANES
    c_all = jnp.zeros((n_rows, d), F32).at[0].set(c_ctx).at[1:1 + dec_batch].set(c)
    mod = _ada_mod(c_all, ada_w, ada_b)
    mods_p = [mod[l, 0:1][:, None, :] for l in range(depth)]
    mods_s = [mod[l, 1:1 + dec_batch][:, None, :] for l in range(depth)]

    tm = 256
    final_g = final_norm_g[None]

    ident = jnp.concatenate([jnp.ones((tm, QK_ROPE), F32), jnp.zeros((tm, QK_ROPE), F32)], axis=1)
    y_p, ckvs, krs = _run_stream(x_prompt.reshape(batch * seq, d), mods_p, lws, ew, final_g, batch=batch, seq=seq, tab=ident,
                                 cache_k=None, cache_v=None, merge_moe_rows=True, tm=tm, tq=min(seq, 256), tk=min(seq, 512), n_hd=N_HEADS, ahead=1)

    ckv_c = jnp.transpose(cache_ckv, (1, 0, 2, 3))
    kr_c = jnp.transpose(cache_krope, (1, 0, 2, 3))
    kr_c = jnp.pad(kr_c, ((0, 0), (0, 0), (0, 0), (0, LANES - kr_c.shape[-1])))
    ck, cvv = _cache_kv(ckv_c, kr_c, jnp.stack([lw["w_uk"] for lw in lws]), jnp.stack([lw["w_vt"] for lw in lws]))
    y_s, _, _ = _run_stream(x_sample.reshape(dec_batch * dec_seq, d), mods_s, lws, ew, final_g, batch=dec_batch, seq=dec_seq,
                            tab=_rope_table(dec_seq), cache_k=ck, cache_v=cvv, merge_moe_rows=False, tm=tm, tq=256, tk=min(512, dec_seq // 4), n_hd=4, ahead=3)

    new_ckv = jnp.stack([a.reshape(batch, seq, -1) for a in ckvs], axis=1)
    new_kr = jnp.stack([a.reshape(batch, seq, -1) for a in krs], axis=1)
    return (y_p.reshape(batch, seq, d), y_s.reshape(dec_batch, dec_seq, d), new_ckv, new_kr)
```

```python
import functools

import jax
import jax.numpy as jnp
from jax import lax
from jax.experimental import pallas as pl
from jax.experimental.pallas import tpu as pltpu

LANES = 128
SUBLANES = 8
BF16_SUBLANES = 16
VMEM_LIMIT_BYTES = 56 * 1024 * 1024

N_HEADS = 8
QK_NOPE = 128
QK_ROPE = 64
V_DIM = 128
HEAD_SLAB = 256
N_EXPERTS = 16
EC_CAPACITY = 2
GRID_W = 64
SC_WIDTH = 3
CF_WIDTH = 31
CF_PAD = (CF_WIDTH - 1) // 2
HALO = 16
CONV_ROWS = 64
HG_PITCH = 24
ROPE_BASE = 10000.0
NORM_EPS = 1e-6
ATTN_SCALE = (QK_NOPE + QK_ROPE) ** -0.5
LOG2_E = 1.4426950408889634
Q_SCALE = ATTN_SCALE * LOG2_E

F32 = jnp.float32
BF16 = jnp.bfloat16


def _params(sem):
    return pltpu.CompilerParams(dimension_semantics=sem, vmem_limit_bytes=VMEM_LIMIT_BYTES)


def _sigmoid(x):
    return 1.0 / (1.0 + jnp.exp(-x))


def _rms(x, g):
    return x * lax.rsqrt(jnp.mean(x * x, axis=-1, keepdims=True) + NORM_EPS) * g


def _split3_dot(a, b, dims):
    a_hi = a.astype(BF16)
    a_lo = (a - a_hi.astype(F32)).astype(BF16)
    b_hi = b.astype(BF16)
    b_lo = (b - b_hi.astype(F32)).astype(BF16)
    dot = functools.partial(lax.dot_general, dimension_numbers=(dims, ((), ())), preferred_element_type=F32)
    return dot(a_hi, b_hi) + dot(a_lo, b_hi) + dot(a_hi, b_lo)


def _ada_kernel(c_ref, w_ref, b_ref, o_ref):
    cv = c_ref[...]
    s = cv * _sigmoid(cv)
    o_ref[0] = _split3_dot(s, w_ref[0], ((1,), (0,))) + b_ref[0]


def _ada_mod(c_all, ada_w, ada_b):
    depth, d, n6 = ada_w.shape
    rows = c_all.shape[0]
    tn = 1024
    return pl.pallas_call(
        _ada_kernel,
        grid=(depth, n6 // tn),
        in_specs=[pl.BlockSpec((rows, d), lambda l, j: (0, 0)),
                  pl.BlockSpec((1, d, tn), lambda l, j: (l, 0, j)),
                  pl.BlockSpec((1, 1, tn), lambda l, j: (l, 0, j))],
        out_specs=pl.BlockSpec((1, rows, tn), lambda l, j: (l, 0, j)),
        out_shape=jax.ShapeDtypeStruct((depth, rows, n6), F32),
        compiler_params=_params(("parallel", "parallel")),
        name="ada_mod",
    )(c_all, ada_w, ada_b.reshape(depth, 1, n6))


def _moe_residual(x, moe_ref, g2, tm):
    chunks = []
    for p in range(2):
        for c in range(SUBLANES):
            chunks.append(moe_ref[0, p, pl.ds(c, tm, stride=SUBLANES), :])
    return x + g2 * jnp.concatenate(chunks, axis=1)


def _rope_pair(v, tab):
    prod = v * tab
    r = prod + pltpu.roll(prod, QK_ROPE, 1)
    lane = lax.broadcasted_iota(jnp.int32, r.shape, 1)
    return jnp.where(lane < QK_ROPE, r, 0.0)


def _inproj_kernel(has_moe, tm, d_model, q_lora, kv_lora, *refs):
    if has_moe:
        (x_ref, moe_ref, g2_ref, shsc_ref, n1_ref, tab_ref, win_ref, qg_ref, wuq_ref, kvg_ref, wuk_ref, wvt_ref,
         xo_ref, q_ref, k_ref, vt_ref, cv_ref, ckv_ref, kr_ref) = refs
        x = _moe_residual(x_ref[...], moe_ref, g2_ref[0], tm)
        xo_ref[...] = x
    else:
        (x_ref, shsc_ref, n1_ref, tab_ref, win_ref, qg_ref, wuq_ref, kvg_ref, wuk_ref, wvt_ref,
         q_ref, k_ref, vt_ref, cv_ref, ckv_ref, kr_ref) = refs
        x = x_ref[...]
    sh1 = shsc_ref[0, :, :d_model]
    sc1 = shsc_ref[0, :, d_model:]
    h = _rms(x, n1_ref[...]) * (1.0 + sc1) + sh1
    u = jnp.dot(h.astype(BF16), win_ref[...], preferred_element_type=F32)
    tab = tab_ref[...]

    cqn = _rms(u[:, :q_lora], qg_ref[...])
    qf = jnp.dot(cqn.astype(BF16), wuq_ref[...], preferred_element_type=F32)
    for hd in range(N_HEADS):
        o = hd * HEAD_SLAB
        q_ref[:, o:o + QK_NOPE] = (qf[:, o:o + QK_NOPE] * Q_SCALE).astype(BF16)
        q_ref[:, o + QK_NOPE:o + HEAD_SLAB] = (_rope_pair(qf[:, o + QK_NOPE:o + HEAD_SLAB], tab) * Q_SCALE).astype(BF16)

    c0 = q_lora
    ckvn = _rms(u[:, c0:c0 + kv_lora], kvg_ref[...])
    ckv_ref[...] = ckvn
    ckvb = ckvn.astype(BF16)
    knf = jnp.dot(ckvb, wuk_ref[...], preferred_element_type=F32)
    vt_ref[0] = lax.dot_general(wvt_ref[...], ckvb, (((1,), (1,)), ((), ())), preferred_element_type=F32).astype(BF16)
    c1 = c0 + kv_lora
    krv = u[:, c1:c1 + LANES]
    kr_ref[...] = krv[:, :QK_ROPE]
    kr2 = _rope_pair(krv, tab).astype(BF16)
    for hd in range(N_HEADS):
        o = hd * HEAD_SLAB
        k_ref[:, o:o + QK_NOPE] = knf[:, hd * QK_NOPE:(hd + 1) * QK_NOPE].astype(BF16)
        k_ref[:, o + QK_NOPE:o + HEAD_SLAB] = kr2
    cv_ref[...] = u[:, c1 + LANES:]


def _inproj(x, moe, mod_prev, mod, n1, tab, lw, *, tm, tiles_per_mod, tiles_per_seq):
    t, d = x.shape
    nt = t // tm
    p_in = lw["w_in"].shape[1]
    q_lora = lw["q_norm_g"].shape[1]
    kv_lora = lw["kv_norm_g"].shape[1]
    n_cv = p_in - q_lora - kv_lora - LANES
    has_moe = moe is not None
    tiles_per_row = None if moe is None else moe.shape[2] // (tm * SUBLANES)

    full = lambda a: pl.BlockSpec(a.shape, lambda i: (0,) * a.ndim)
    row = lambda w: pl.BlockSpec((tm, w), lambda i: (i, 0))
    in_specs = [row(d)]
    args = [x]
    if has_moe:
        in_specs += [pl.BlockSpec((1, 2, tm * SUBLANES, LANES), lambda i: (i // tiles_per_row, 0, i % tiles_per_row, 0)),
                     pl.BlockSpec((1, 1, d), lambda i: (i // tiles_per_mod, 0, 5))]
        args += [moe, mod_prev]
    in_specs += [pl.BlockSpec((1, 1, 2 * d), lambda i: (i // tiles_per_mod, 0, 0)),
                 full(n1),
                 pl.BlockSpec((tm, LANES), lambda i: (i % tiles_per_seq, 0)),
                 full(lw["w_in"]), full(lw["q_norm_g"]), full(lw["w_uq"]), full(lw["kv_norm_g"]), full(lw["w_uk"]), full(lw["w_vt"])]
    args += [mod, n1, tab, lw["w_in"], lw["q_norm_g"], lw["w_uq"], lw["kv_norm_g"], lw["w_uk"], lw["w_vt"]]

    out_specs, out_shape = [], []
    if has_moe:
        out_specs.append(row(d))
        out_shape.append(jax.ShapeDtypeStruct((t, d), F32))
    seq = tiles_per_seq * tm
    for w, dt in ((N_HEADS * HEAD_SLAB, BF16), (N_HEADS * HEAD_SLAB, BF16), (None, BF16),
                  (n_cv, F32), (kv_lora, F32), (QK_ROPE, F32)):
        if w is None:
            out_specs.append(pl.BlockSpec((1, N_HEADS * V_DIM, tm), lambda i: (i // tiles_per_seq, 0, i % tiles_per_seq)))
            out_shape.append(jax.ShapeDtypeStruct((t // seq, N_HEADS * V_DIM, seq), dt))
        else:
            out_specs.append(row(w))
            out_shape.append(jax.ShapeDtypeStruct((t, w), dt))

    outs = pl.pallas_call(
        functools.partial(_inproj_kernel, has_moe, tm, d, q_lora, kv_lora),
        grid=(nt,), in_specs=in_specs, out_specs=out_specs, out_shape=out_shape,
        compiler_params=_params(("parallel",)), name="inproj",
    )(*args)
    if not has_moe:
        outs = [x] + list(outs)
    return outs


def _cachekv_kernel(ckv_ref, kr_ref, wuk_ref, wvt_ref, k_ref, vt_ref):
    ckvb = ckv_ref[0, 0].astype(BF16)
    knf = jnp.dot(ckvb, wuk_ref[0], preferred_element_type=F32)
    kr2 = kr_ref[0, 0].astype(BF16)
    for hd in range(N_HEADS):
        o = hd * HEAD_SLAB
        k_ref[0, 0, :, o:o + QK_NOPE] = knf[:, hd * QK_NOPE:(hd + 1) * QK_NOPE].astype(BF16)
        k_ref[0, 0, :, o + QK_NOPE:o + HEAD_SLAB] = kr2
    vt_ref[0, 0] = lax.dot_general(wvt_ref[0], ckvb, (((1,), (1,)), ((), ())), preferred_element_type=F32).astype(BF16)


def _cache_kv(ckv_c, kr_c, w_uk_all, w_vt_all):
    depth, b, past, kv_lora = ckv_c.shape
    nk = N_HEADS * HEAD_SLAB
    nv = N_HEADS * V_DIM
    return pl.pallas_call(
        _cachekv_kernel,
        grid=(depth, b),
        in_specs=[pl.BlockSpec((1, 1, past, kv_lora), lambda l, i: (l, i, 0, 0)),
                  pl.BlockSpec((1, 1, past, LANES), lambda l, i: (l, i, 0, 0)),
                  pl.BlockSpec((1, kv_lora, N_HEADS * QK_NOPE), lambda l, i: (l, 0, 0)),
                  pl.BlockSpec((1, nv, kv_lora), lambda l, i: (l, 0, 0))],
        out_specs=[pl.BlockSpec((1, 1, past, nk), lambda l, i: (l, i, 0, 0)),
                   pl.BlockSpec((1, 1, nv, past), lambda l, i: (l, i, 0, 0))],
        out_shape=[jax.ShapeDtypeStruct((depth, b, past, nk), BF16), jax.ShapeDtypeStruct((depth, b, nv, past), BF16)],
        compiler_params=_params(("parallel", "parallel")), name="cache_kv",
    )(ckv_c, kr_c, w_uk_all, w_vt_all)


def _attn_kernel(has_ctx, n_hd, ahead, lk, tk, *refs):
    if has_ctx:
        q_ref, k_ref, vt_ref, kc_ref, vct_ref, o_ref, qt_scr, s_scr, sc_scr = refs
    else:
        q_ref, k_ref, vt_ref, o_ref, qt_scr, s_scr = refs
    tq = q_ref.shape[0]
    n = lk // tk
    for g in range(n_hd):
        qt_scr[g] = jnp.transpose(q_ref[:, g * HEAD_SLAB:(g + 1) * HEAD_SLAB].astype(F32)).astype(BF16)

    def scores(g, kc, s_ref):
        s = jnp.dot(kc, qt_scr[g], preferred_element_type=F32)
        s_ref[...] = s
        return jnp.max(s, axis=0, keepdims=True)

    def k_chunk(g, c):
        return k_ref[c * tk:(c + 1) * tk, g * HEAD_SLAB:(g + 1) * HEAD_SLAB]

    def vt_chunk(g, c):
        return vt_ref[0, g * V_DIM:(g + 1) * V_DIM, c * tk:(c + 1) * tk]

    def accumulate(state, s_ref, smax, vt):
        m, l, acc = state
        m_new = jnp.maximum(m, smax)
        alpha = jnp.exp2(m - m_new)
        p = jnp.exp2(s_ref[...] - m_new)
        l = alpha * l + jnp.sum(p, axis=0, keepdims=True)
        acc = alpha * acc + jnp.dot(vt, p.astype(BF16), preferred_element_type=F32)
        return m_new, l, acc

    total = n + (1 if has_ctx else 0)
    n_slots = s_scr.shape[1]

    def s_of(g, c):
        return s_scr.at[g, c % n_slots] if c < n else sc_scr.at[g]

    def issue(g, c):
        kc = k_chunk(g, c) if c < n else kc_ref[0, :, g * HEAD_SLAB:(g + 1) * HEAD_SLAB]
        return scores(g, kc, s_of(g, c))

    def vt_of(g, c):
        return vt_chunk(g, c) if c < n else vct_ref[0, g * V_DIM:(g + 1) * V_DIM, :]

    states = [(jnp.full((1, tq), -jnp.inf, F32), jnp.zeros((1, tq), F32), jnp.zeros((V_DIM, tq), F32))] * n_hd
    smax = {}
    for c in range(min(ahead, total)):
        for g in range(n_hd):
            smax[g, c] = issue(g, c)
    for c in range(total):
        for g in range(n_hd):
            if c + ahead < total:
                smax[g, c + ahead] = issue(g, c + ahead)
            states[g] = accumulate(states[g], s_of(g, c), smax.pop((g, c)), vt_of(g, c))
    for g in range(n_hd):
        _, l, acc = states[g]
        o_ref[:, g * V_DIM:(g + 1) * V_DIM] = jnp.transpose(acc / l).astype(BF16)


def _attention(q, k, vt, kc, vct, *, batch, seq, tq, tk, n_hd, ahead):
    t = q.shape[0]
    nq = seq // tq
    has_ctx = kc is not None
    in_specs = [pl.BlockSpec((tq, n_hd * HEAD_SLAB), lambda b, h, i: (b * nq + i, h)),
                pl.BlockSpec((seq, n_hd * HEAD_SLAB), lambda b, h, i: (b, h)),
                pl.BlockSpec((1, n_hd * V_DIM, seq), lambda b, h, i: (b, h, 0))]
    args = [q, k, vt]
    scratch = [pltpu.VMEM((n_hd, HEAD_SLAB, tq), BF16), pltpu.VMEM((n_hd, ahead + 1, tk, tq), F32)]
    if has_ctx:
        past = kc.shape[1]
        in_specs += [pl.BlockSpec((1, past, n_hd * HEAD_SLAB), lambda b, h, i: (b, 0, h)),
                     pl.BlockSpec((1, n_hd * V_DIM, past), lambda b, h, i: (b, h, 0))]
        args += [kc, vct]
        scratch += [pltpu.VMEM((n_hd, past, tq), F32)]
    return pl.pallas_call(
        functools.partial(_attn_kernel, has_ctx, n_hd, ahead, seq, tk),
        grid=(batch, N_HEADS // n_hd, nq), in_specs=in_specs,
        out_specs=pl.BlockSpec((tq, n_hd * V_DIM), lambda b, h, i: (b * nq + i, h)),
        out_shape=jax.ShapeDtypeStruct((t, N_HEADS * V_DIM), BF16),
        scratch_shapes=scratch,
        compiler_params=_params(("parallel", "parallel", "arbitrary")), name="attention",
    )(*args)


def _mixout_kernel(tm, tiles_per_seq, d_sc, d_cf,
                   o_ref, cv_ref, cvp_ref, cvn_ref, x_ref, g1_ref, sh2_ref, sc2_ref,
                   scw_ref, cfw_ref, cfb_ref, lng_ref, lnb_ref, wo_ref, n2_ref, rw_ref,
                   xo_ref, hg_ref, lg_ref, ext_sc, ext_cf, ph_scr, z_scr, hg_scr):
    i = pl.program_id(0)
    first = (i % tiles_per_seq) == 0
    last = (i % tiles_per_seq) == tiles_per_seq - 1
    o_scx, o_scb, o_scc, o_cfa, o_cfg = 0, d_sc, 2 * d_sc, 3 * d_sc, 3 * d_sc + d_cf

    @pl.when(i == 0)
    def _():
        hg_scr[...] = jnp.zeros_like(hg_scr)

    def sc_in(ref):
        return ref[:, o_scc:o_scc + d_sc] * ref[:, o_scx:o_scx + d_sc]

    def cf_in(ref):
        return ref[:, o_cfa:o_cfa + d_cf] * _sigmoid(ref[:, o_cfg:o_cfg + d_cf])

    ext_sc[0:HALO, :] = jnp.where(first, 0.0, sc_in(cvp_ref))
    ext_sc[HALO:HALO + tm, :] = sc_in(cv_ref)
    ext_sc[HALO + tm:, :] = jnp.where(last, 0.0, sc_in(cvn_ref))
    ext_cf[0:HALO, :] = jnp.where(first, 0.0, cf_in(cvp_ref))
    ext_cf[HALO:HALO + tm, :] = cf_in(cv_ref)
    ext_cf[HALO + tm:, :] = jnp.where(last, 0.0, cf_in(cvn_ref))

    sc_pad = (SC_WIDTH - 1) // 2
    y_sc = jnp.zeros((tm, d_sc), F32)
    for kk in range(SC_WIDTH):
        y_sc = y_sc + scw_ref[kk:kk + 1, :] * ext_sc[pl.ds(HALO - sc_pad + kk, tm), :]
    y_sc = cv_ref[:, o_scb:o_scb + d_sc] * y_sc

    first_row = HALO - CF_PAD
    span = tm + 2 * HALO - SUBLANES
    for phase in range(1, SUBLANES):
        ph_scr[phase - 1, 0:span, :] = ext_cf[pl.ds(phase, span), :]
    for ct in range(d_cf // LANES):
        cs = slice(ct * LANES, (ct + 1) * LANES)
        for rb in range(tm // CONV_ROWS):
            acc = jnp.zeros((CONV_ROWS, LANES), F32)
            for kk in range(CF_WIDTH):
                phase = (first_row + kk) % SUBLANES
                base = first_row + kk - phase + rb * CONV_ROWS
                src = ext_cf if phase == 0 else ph_scr.at[phase - 1]
                acc = acc + cfw_ref[kk:kk + 1, cs] * src[base:base + CONV_ROWS, cs]
            z_scr[rb * CONV_ROWS:(rb + 1) * CONV_ROWS, cs] = acc
    z = z_scr[...] + cfb_ref[...]
    mu = jnp.mean(z, axis=-1, keepdims=True)
    zc = z - mu
    var = jnp.mean(zc * zc, axis=-1, keepdims=True)
    z = zc * lax.rsqrt(var + NORM_EPS) * lng_ref[...] + lnb_ref[...]
    z = z * _sigmoid(z)

    mixed = jnp.concatenate([o_ref[...], y_sc.astype(BF16), z.astype(BF16)], axis=1)
    y = jnp.dot(mixed, wo_ref[...], preferred_element_type=F32)
    x = x_ref[...] + g1_ref[0] * y
    xo_ref[...] = x
    h2 = _rms(x, n2_ref[...]) * (1.0 + sc2_ref[0]) + sh2_ref[0]

    lg = _split3_dot(h2, rw_ref[...], ((1,), (0,)))
    lg_ref[0] = jnp.transpose(lg)[:N_EXPERTS]

    d_model = h2.shape[1]
    for c in range(d_model // LANES):
        hg_scr[pl.ds(c, tm, stride=HG_PITCH), :] = h2[:, c * LANES:(c + 1) * LANES]
    tiles = hg_scr[...].reshape(tm, HG_PITCH, LANES)[:, :BF16_SUBLANES, :]
    hg_ref[...] = tiles.reshape(tm * BF16_SUBLANES, LANES).astype(BF16)


def _mixout(o_att, cv, x, mod, lw, *, tm, tiles_per_mod, tiles_per_seq, route_batch, route_len):
    t, d = x.shape
    nt = t // tm
    n_cv = cv.shape[1]
    d_sc = lw["sc_conv_w"].shape[1]
    d_cf = lw["cf_conv_w"].shape[1]
    hb = tm // HALO
    n_halo = t // HALO
    tiles_per_route = route_len // tm
    full = lambda a: pl.BlockSpec(a.shape, lambda i: (0,) * a.ndim)
    row = lambda w: pl.BlockSpec((tm, w), lambda i: (i, 0))
    modspec = lambda blk: pl.BlockSpec((1, 1, d), lambda i: (i // tiles_per_mod, 0, blk))
    in_specs = [row(o_att.shape[1]), row(n_cv),
                pl.BlockSpec((HALO, n_cv), lambda i: (jnp.maximum(i * hb - 1, 0), 0)),
                pl.BlockSpec((HALO, n_cv), lambda i: (jnp.minimum((i + 1) * hb, n_halo - 1), 0)),
                row(d), modspec(2), modspec(3), modspec(4),
                full(lw["sc_conv_w"]), full(lw["cf_conv_w"]), full(lw["cf_conv_b"]), full(lw["cf_ln_g"]),
                full(lw["cf_ln_b"]), full(lw["w_o"]), full(lw["norm2_g"]), full(lw["router_w"])]
    args = [o_att, cv, cv, cv, x, mod, mod, mod, lw["sc_conv_w"], lw["cf_conv_w"], lw["cf_conv_b"], lw["cf_ln_g"],
            lw["cf_ln_b"], lw["w_o"], lw["norm2_g"], lw["router_w"]]
    out_specs = [row(d),
                 pl.BlockSpec((tm * BF16_SUBLANES, LANES), lambda i: (i, 0)),
                 pl.BlockSpec((1, N_EXPERTS, tm), lambda i: (i // tiles_per_route, 0, i % tiles_per_route))]
    out_shape = [jax.ShapeDtypeStruct((t, d), F32),
                 jax.ShapeDtypeStruct((t * BF16_SUBLANES, LANES), BF16),
                 jax.ShapeDtypeStruct((route_batch, N_EXPERTS, route_len), F32)]
    return pl.pallas_call(
        functools.partial(_mixout_kernel, tm, tiles_per_seq, d_sc, d_cf),
        grid=(nt,), in_specs=in_specs, out_specs=out_specs, out_shape=out_shape,
        scratch_shapes=[pltpu.VMEM((tm + 2 * HALO, d_sc), F32), pltpu.VMEM((tm + 2 * HALO, d_cf), F32),
                        pltpu.VMEM((SUBLANES - 1, tm + 2 * HALO, d_cf), F32), pltpu.VMEM((tm, d_cf), F32),
                        pltpu.VMEM((tm * HG_PITCH, LANES), F32)],
        compiler_params=_params(("arbitrary",)), name="mixout",
    )(*args)


def _route_kernel(seq, cap, jb, lg_ref, idx_ref, gate_ref, slot_scr, aff_scr, hit_scr):
    lg = lg_ref[0]
    ex = jnp.exp(lg - jnp.max(lg, axis=0, keepdims=True))
    aff = ex / jnp.sum(ex, axis=0, keepdims=True)
    bits = pltpu.bitcast(aff, jnp.int32)
    tok = lax.broadcasted_iota(jnp.int32, aff.shape, 1)
    capf = jnp.float32(cap)

    def count(mask):
        return jnp.sum(jnp.where(mask, 1.0, 0.0), axis=1, keepdims=True)

    def greedy_bits(n_bits, keep):
        v = jnp.zeros((N_EXPERTS, 1), jnp.int32)
        bit = n_bits - 1
        while bit >= 0:
            hi = jnp.int32(1 << bit)
            if bit == 0:
                v = jnp.where(keep(v | hi), v | hi, v)
                bit -= 1
            else:
                lo = jnp.int32(1 << (bit - 1))
                k11, k10, k01 = keep(v | hi | lo), keep(v | hi), keep(v | lo)
                v = jnp.where(k11, v | hi | lo, jnp.where(k10, v | hi, jnp.where(k01, v | lo, v)))
                bit -= 2
        return v

    thr = greedy_bits(31, lambda cand: count(bits >= cand) >= capf)
    gt = bits > thr
    eq = bits == thr
    need = capf - count(gt)
    cut = greedy_bits(max(seq.bit_length(), 1), lambda cand: count(eq & (tok < cand)) <= need)
    sel = gt | (eq & (tok < cut))

    r_i = lax.broadcasted_iota(jnp.int32, (LANES, LANES), 0)
    c_i = lax.broadcasted_iota(jnp.int32, (LANES, LANES), 1)
    tri = jnp.where(r_i < c_i, 1.0, 0.0).astype(BF16)
    self = jnp.where(sel, 1.0, 0.0)
    carry = jnp.zeros((N_EXPERTS, 1), F32)
    slots = []
    for blk in range(seq // LANES):
        seg = self[:, blk * LANES:(blk + 1) * LANES]
        excl = jnp.dot(seg.astype(BF16), tri, preferred_element_type=F32)
        slots.append(jnp.where(seg > 0.0, excl + carry, -1.0))
        carry = carry + jnp.sum(seg, axis=1, keepdims=True)
    slot_scr[...] = jnp.concatenate(slots, axis=1)
    aff_scr[...] = aff

    n_chunks = seq // LANES
    jcol = lax.broadcasted_iota(jnp.int32, (jb, LANES), 0).astype(F32)
    lane = lax.broadcasted_iota(jnp.int32, (jb, LANES), 1).astype(F32)

    n_blk = cap // jb
    lead_want = lax.broadcasted_iota(jnp.int32, (n_blk, LANES), 0).astype(F32) * float(jb)
    lead_lane = lax.broadcasted_iota(jnp.int32, (n_blk, LANES), 1).astype(F32)

    for e in range(N_EXPERTS):
        def lead_chunk(cb, acc, e=e):
            off = pl.multiple_of(cb * LANES, LANES)
            hit = slot_scr[e:e + 1, pl.ds(off, LANES)] == lead_want
            return acc + jnp.where(hit, lead_lane + off.astype(F32), 0.0)

        lead = lax.fori_loop(0, n_chunks, lead_chunk, jnp.zeros((n_blk, LANES), F32), unroll=min(n_chunks, 4))
        lead_tok = jnp.sum(lead, axis=1, keepdims=True).astype(jnp.int32)

        for blk in range(n_blk):
            j0 = blk * jb
            want = jcol + float(j0)
            first = lead_tok[blk, 0] // LANES
            last = lead_tok[blk + 1, 0] // LANES if blk + 1 < n_blk else n_chunks - 1

            def per_chunk(cb, carry, e=e, want=want):
                idx_acc, g_acc = carry
                off = pl.multiple_of(cb * LANES, LANES)
                hit = slot_scr[e:e + 1, pl.ds(off, LANES)] == want
                idx_acc = idx_acc + jnp.where(hit, lane + off.astype(F32), 0.0)
                g_acc = g_acc + jnp.where(hit, aff_scr[e:e + 1, pl.ds(off, LANES)], 0.0)
                return idx_acc, g_acc

            zero = jnp.zeros((jb, LANES), F32)
            idx_acc, g_acc = lax.fori_loop(first, last + 1, per_chunk, (zero, zero))
            hit_scr[0, j0:j0 + jb, :] = idx_acc
            hit_scr[1, j0:j0 + jb, :] = g_acc
        idx_ref[0, e] = jnp.sum(hit_scr[0], axis=1, keepdims=True).astype(jnp.int32)
        gate_ref[0, e] = jnp.sum(hit_scr[1], axis=1, keepdims=True)


def _route(logits_t, cap):
    b, e, seq = logits_t.shape
    jb = min(cap, 64)
    return pl.pallas_call(
        functools.partial(_route_kernel, seq, cap, jb),
        grid=(b,),
        in_specs=[pl.BlockSpec((1, e, seq), lambda i: (i, 0, 0))],
        out_specs=[pl.BlockSpec((1, e, cap, 1), lambda i: (i, 0, 0, 0)),
                   pl.BlockSpec((1, e, cap, 1), lambda i: (i, 0, 0, 0))],
        out_shape=[jax.ShapeDtypeStruct((b, e, cap, 1), jnp.int32), jax.ShapeDtypeStruct((b, e, cap, 1), F32)],
        scratch_shapes=[pltpu.VMEM((e, seq), F32), pltpu.VMEM((e, seq), F32), pltpu.VMEM((2, cap, LANES), F32)],
        compiler_params=_params(("parallel",)), name="route",
    )(logits_t)


def _moe_up_kernel(cap, d_model, pitch, idx_ref, hg_ref, wg_ref, wu_ref, hid_ref, xg_scr, xs_scr):
    b, e = pl.program_id(0), pl.program_id(1)

    def gather(expert, slot, j):
        t = idx_ref[(b * N_EXPERTS + expert) * cap + j]
        src = pl.multiple_of(t * BF16_SUBLANES, BF16_SUBLANES)
        tile = hg_ref[0, pl.ds(src, BF16_SUBLANES), :].astype(F32)
        xg_scr[slot, pl.ds(j, SUBLANES, stride=pitch), :] = tile[:SUBLANES]
        xg_scr[slot, pl.ds(SUBLANES * pitch + j, SUBLANES, stride=pitch), :] = tile[SUBLANES:]

    @pl.when(e == 0)
    def _():
        def first(j, _):
            gather(0, 0, j)
            return 0
        lax.fori_loop(0, cap, first, 0, unroll=8)

    slot = e % 2
    for c in range(d_model // LANES):
        xs_scr[:, c * LANES:(c + 1) * LANES] = xg_scr[slot, c * pitch:c * pitch + cap, :].astype(BF16)
    xs = xs_scr[...]
    g = jnp.dot(xs, wg_ref[0, 0], preferred_element_type=F32)
    u = jnp.dot(xs, wu_ref[0, 0], preferred_element_type=F32)
    hid_ref[0, 0] = (g * _sigmoid(g) * u).astype(BF16)
    nxt = jnp.minimum(e + 1, N_EXPERTS - 1)
    for j in range(cap):
        gather(nxt, 1 - slot, j)


def _moe_up(idx_flat, hg, wg, wu, *, layer, cap):
    b, rows, _ = hg.shape
    _, n_e, d_model, d_ff = wg.shape
    pitch = cap + SUBLANES
    grid_spec = pltpu.PrefetchScalarGridSpec(
        num_scalar_prefetch=1, grid=(b, n_e),
        in_specs=[pl.BlockSpec((1, rows, LANES), lambda i, e, idx: (i, 0, 0), pipeline_mode=pl.Buffered(1)),
                  pl.BlockSpec((1, 1, d_model, d_ff), lambda i, e, idx: (layer, e, 0, 0)),
                  pl.BlockSpec((1, 1, d_model, d_ff), lambda i, e, idx: (layer, e, 0, 0))],
        out_specs=pl.BlockSpec((1, 1, cap, d_ff), lambda i, e, idx: (i, e, 0, 0)),
        scratch_shapes=[pltpu.VMEM((2, d_model // LANES * pitch, LANES), F32), pltpu.VMEM((cap, d_model), BF16)])
    return pl.pallas_call(
        functools.partial(_moe_up_kernel, cap, d_model, pitch),
        grid_spec=grid_spec,
        out_shape=jax.ShapeDtypeStruct((b, n_e, cap, d_ff), BF16),
        compiler_params=_params(("arbitrary", "arbitrary")), name="moe_up",
    )(idx_flat, hg, wg, wu)


def _moe_down_kernel(cap, half, pitch, group, idx_ref, hid_ref, gate_ref, wd_ref, acc_ref, og_scr):
    b, e = pl.program_id(0), pl.program_id(2)

    def project(slot):
        o = jnp.dot(hid_ref[0, 0], wd_ref[0, 0], preferred_element_type=F32) * gate_ref[0]
        for c in range(half // LANES):
            og_scr[slot, c * pitch:c * pitch + cap, :] = o[:, c * LANES:(c + 1) * LANES]

    def scatter(expert, slot, j0):
        base = (b * N_EXPERTS + expert) * cap
        dsts = [pl.multiple_of(idx_ref[base + j0 + k], SUBLANES) for k in range(group)]
        vals = [acc_ref[0, 0, pl.ds(dsts[k], SUBLANES), :] + og_scr[slot, pl.ds(j0 + k, SUBLANES, stride=pitch), :]
                for k in range(group)]
        for k in range(group):
            acc_ref[0, 0, pl.ds(dsts[k], SUBLANES), :] = vals[k]

    slot = e % 2

    @pl.when(e == 0)
    def _():
        acc_ref[...] = jnp.zeros_like(acc_ref)
        project(slot)

    @pl.when(e > 0)
    def _():
        for jg in range(cap // group):
            scatter(e - 1, 1 - slot, jg * group)
        project(slot)

    @pl.when(e == N_EXPERTS - 1)
    def _():
        def flush(jg, _):
            scatter(e, slot, jg * group)
            return 0
        lax.fori_loop(0, cap // group, flush, 0)


def _moe_down(idx_flat, hid, gate, wd, *, layer, seq):
    b, n_e, cap, d_ff = hid.shape
    d_model = wd.shape[3]
    half = d_model // 2
    pitch = cap + SUBLANES
    group = min(cap, 8)
    grid_spec = pltpu.PrefetchScalarGridSpec(
        num_scalar_prefetch=1, grid=(b, 2, n_e),
        in_specs=[pl.BlockSpec((1, 1, cap, d_ff), lambda i, p, e, idx: (i, e, 0, 0)),
                  pl.BlockSpec((1, cap, 1), lambda i, p, e, idx: (i * n_e + e, 0, 0)),
                  pl.BlockSpec((1, 1, d_ff, half), lambda i, p, e, idx: (layer, e, 0, p))],
        out_specs=pl.BlockSpec((1, 1, seq * SUBLANES, LANES), lambda i, p, e, idx: (i, p, 0, 0)),
        scratch_shapes=[pltpu.VMEM((2, half // LANES * pitch, LANES), F32)])
    return pl.pallas_call(
        functools.partial(_moe_down_kernel, cap, half, pitch, group),
        grid_spec=grid_spec,
        out_shape=jax.ShapeDtypeStruct((b, 2, seq * SUBLANES, LANES), F32),
        compiler_params=_params(("arbitrary", "arbitrary", "arbitrary")), name="moe_down",
    )(idx_flat, hid, gate, wd)


def _final_kernel(tm, x_ref, moe_ref, g2_ref, g_ref, y_ref):
    x = _moe_residual(x_ref[...], moe_ref, g2_ref[0], tm)
    y_ref[...] = _rms(x, g_ref[...])


def _final(x, moe, mod_prev, g, *, tm, tiles_per_mod):
    t, d = x.shape
    tiles_per_row = moe.shape[2] // (tm * SUBLANES)
    return pl.pallas_call(
        functools.partial(_final_kernel, tm),
        grid=(t // tm,),
        in_specs=[pl.BlockSpec((tm, d), lambda i: (i, 0)),
                  pl.BlockSpec((1, 2, tm * SUBLANES, LANES), lambda i: (i // tiles_per_row, 0, i % tiles_per_row, 0)),
                  pl.BlockSpec((1, 1, d), lambda i: (i // tiles_per_mod, 0, 5)),
                  pl.BlockSpec(g.shape, lambda i: (0, 0))],
        out_specs=pl.BlockSpec((tm, d), lambda i: (i, 0)),
        out_shape=jax.ShapeDtypeStruct((t, d), F32),
        compiler_params=_params(("parallel",)), name="final_norm",
    )(x, moe, mod_prev, g)


def _rope_swap_perm():
    q = QK_ROPE // 4
    return jnp.array(list(range(q, 2 * q)) + list(range(0, q)) + list(range(3 * q, 4 * q)) + list(range(2 * q, 3 * q)),
                     jnp.int32)


def _rope_table(n_tokens):
    rows = n_tokens // GRID_W
    row = jnp.repeat(jnp.arange(rows, dtype=F32), GRID_W)
    col = jnp.tile(jnp.arange(GRID_W, dtype=F32), rows)
    n_pairs = QK_ROPE // 4
    inv = ROPE_BASE ** (-jnp.arange(n_pairs, dtype=F32) / n_pairs)
    ang_r = row[:, None] * inv
    ang_c = col[:, None] * inv
    cr, sr, cc, sn = jnp.cos(ang_r), jnp.sin(ang_r), jnp.cos(ang_c), jnp.sin(ang_c)
    return jnp.concatenate([cr, cr, cc, cc, -sr, sr, -sn, sn], axis=1)


def _layer_weights(l, w):
    d = w["w_in"].shape[1]
    q_lora = w["q_norm_g"].shape[1]
    kv_lora = w["kv_norm_g"].shape[1]
    perm = _rope_swap_perm()
    w_in = w["w_in"][l]
    c1 = q_lora + kv_lora
    kr_w = w_in[:, c1:c1 + QK_ROPE]
    w_in_p = jnp.concatenate([w_in[:, :c1 + QK_ROPE], kr_w[:, perm], w_in[:, c1 + QK_ROPE:]], axis=1).astype(BF16)
    w_uq = w["w_uq"][l]
    rope_w = w_uq[:, :, QK_NOPE:]
    w_uq_p = jnp.concatenate([w_uq, rope_w[:, :, perm]], axis=2).reshape(q_lora, N_HEADS * HEAD_SLAB).astype(BF16)
    w_ukv = w["w_ukv"][l]
    w_uk_p = w_ukv[:, :, :QK_NOPE].reshape(kv_lora, -1).astype(BF16)
    w_vt_p = w_ukv[:, :, QK_NOPE:].reshape(kv_lora, -1).T.astype(BF16)
    return dict(
        w_in=w_in_p, q_norm_g=w["q_norm_g"][l][None], w_uq=w_uq_p, kv_norm_g=w["kv_norm_g"][l][None], w_uk=w_uk_p, w_vt=w_vt_p,
        norm1_g=w["norm1_g"][l][None], norm2_g=w["norm2_g"][l][None],
        sc_conv_w=w["sc_conv_w"][l], cf_conv_w=w["cf_conv_w"][l], cf_conv_b=w["cf_conv_b"][l][None],
        cf_ln_g=w["cf_ln_g"][l][None], cf_ln_b=w["cf_ln_b"][l][None],
        w_o=w["w_o"][l].astype(BF16), router_w=jnp.pad(w["router_w"][l], ((0, 0), (0, LANES - N_EXPERTS))),
        layer=l)


def _run_stream(x0, mods, lws, ew, final_g, *, batch, seq, tab, cache_k, cache_v, merge_moe_rows, tm, tq, tk, n_hd, ahead):
    t, d = x0.shape
    n_mod = mods[0].shape[0]
    tiles_per_mod = t // n_mod // tm
    tiles_per_seq = seq // tm
    cap = max(1, EC_CAPACITY * seq // N_EXPERTS)
    moe_b, moe_seq = (1, t) if merge_moe_rows else (batch, seq)
    moe_cap = cap * (batch // moe_b)
    x, moe = x0, None
    ckvs, krs = [], []
    for l, lw in enumerate(lws):
        x, q, k, v, cv, ckvn, kr = _inproj(x, moe, mods[l - 1] if l else None, mods[l], lw["norm1_g"], tab, lw,
                                           tm=tm, tiles_per_mod=tiles_per_mod, tiles_per_seq=tiles_per_seq)
        ckvs.append(ckvn)
        krs.append(kr)
        kc = None if cache_k is None else cache_k[l]
        vc = None if cache_v is None else cache_v[l]
        o_att = _attention(q, k, v, kc, vc, batch=batch, seq=seq, tq=tq, tk=tk, n_hd=n_hd, ahead=ahead)
        x, hg, logits_t = _mixout(o_att, cv, x, mods[l], lw, tm=tm, tiles_per_mod=tiles_per_mod,
                                  tiles_per_seq=tiles_per_seq, route_batch=batch, route_len=seq)
        idx, gate = _route(logits_t, cap)
        if moe_b != batch:
            offs = (jnp.arange(batch, dtype=jnp.int32) * seq)[:, None, None, None]
            idx = jnp.transpose(idx + offs, (1, 0, 2, 3)).reshape(1, N_EXPERTS, moe_cap, 1)
            gate = jnp.transpose(gate, (1, 0, 2, 3)).reshape(1, N_EXPERTS, moe_cap, 1)
        idx_flat = idx.reshape(-1)
        gate = gate.reshape(moe_b * N_EXPERTS, moe_cap, 1)
        hid = _moe_up(idx_flat, hg.reshape(moe_b, moe_seq * BF16_SUBLANES, LANES), ew["wg"], ew["wu"], layer=lw["layer"], cap=moe_cap)
        moe = _moe_down(idx_flat * SUBLANES, hid, gate, ew["wd"], layer=lw["layer"], seq=moe_seq)
    y = _final(x, moe, mods[-1], final_g, tm=tm, tiles_per_mod=tiles_per_mod)
    return y, ckvs, krs


def kernel(x_prompt, x_sample, cache_ckv, cache_krope, c, c_ctx, ada_w, ada_b, norm1_g, norm2_g, w_in, q_norm_g, w_uq,
           kv_norm_g, w_ukv, sc_conv_w, cf_conv_w, cf_conv_b, cf_ln_g, cf_ln_b, w_o, router_w, exp_w_gate, exp_w_up,
           exp_w_down, final_norm_g):
    batch, seq, d = x_prompt.shape
    dec_batch, dec_seq, _ = x_sample.shape
    depth = ada_w.shape[0]
    past = cache_ckv.shape[2]
    w = dict(norm1_g=norm1_g, norm2_g=norm2_g, w_in=w_in, q_norm_g=q_norm_g, w_uq=w_uq, kv_norm_g=kv_norm_g, w_ukv=w_ukv,
             sc_conv_w=sc_conv_w, cf_conv_w=cf_conv_w, cf_conv_b=cf_conv_b, cf_ln_g=cf_ln_g, cf_ln_b=cf_ln_b, w_o=w_o,
             router_w=router_w, exp_w_gate=exp_w_gate, exp_w_up=exp_w_up, exp_w_down=exp_w_down)
    lws = [_layer_weights(l, w) for l in range(depth)]
    ew = dict(wg=exp_w_gate.astype(BF16), wu=exp_w_up.astype(BF16), wd=exp_w_down.astype(BF16))

    n_rows = -(-(1 + dec_batch) // SUBLANES) * SUBLANES
    c_all = jnp.zeros((n_rows, d), F32).at[0].set(c_ctx).at[1:1 + dec_batch].set(c)
    mod = _ada_mod(c_all, ada_w, ada_b)
    mods_p = [mod[l, 0:1][:, None, :] for l in range(depth)]
    mods_s = [mod[l, 1:1 + dec_batch][:, None, :] for l in range(depth)]

    tm = 256
    final_g = final_norm_g[None]

    ident = jnp.concatenate([jnp.ones((tm, QK_ROPE), F32), jnp.zeros((tm, QK_ROPE), F32)], axis=1)
    y_p, ckvs, krs = _run_stream(x_prompt.reshape(batch * seq, d), mods_p, lws, ew, final_g, batch=batch, seq=seq, tab=ident,
                                 cache_k=None, cache_v=None, merge_moe_rows=True, tm=tm, tq=min(seq, 256), tk=min(seq, 512), n_hd=N_HEADS, ahead=1)

    ckv_c = jnp.transpose(cache_ckv, (1, 0, 2, 3))
    kr_c = jnp.transpose(cache_krope, (1, 0, 2, 3))
    kr_c = jnp.pad(kr_c, ((0, 0), (0, 0), (0, 0), (0, LANES - kr_c.shape[-1])))
    ck, cvv = _cache_kv(ckv_c, kr_c, jnp.stack([lw["w_uk"] for lw in lws]), jnp.stack([lw["w_vt"] for lw in lws]))
    y_s, _, _ = _run_stream(x_sample.reshape(dec_batch * dec_seq, d), mods_s, lws, ew, final_g, batch=dec_batch, seq=dec_seq,
                            tab=_rope_table(dec_seq), cache_k=ck, cache_v=cvv, merge_moe_rows=False, tm=tm, tq=256, tk=min(512, dec_seq // 4), n_hd=4, ahead=3)

    new_ckv = jnp.stack([a.reshape(batch, seq, -1) for a in ckvs], axis=1)
    new_kr = jnp.stack([a.reshape(batch, seq, -1) for a in krs], axis=1)
    return (y_p.reshape(batch, seq, d), y_s.reshape(dec_batch, dec_seq, d), new_ckv, new_kr)
```

```python
import functools

import jax
import jax.numpy as jnp
from jax import lax
from jax.experimental import pallas as pl
from jax.experimental.pallas import tpu as pltpu

LANES = 128
SUBLANES = 8
BF16_SUBLANES = 16
VMEM_LIMIT_BYTES = 56 * 1024 * 1024

N_HEADS = 8
QK_NOPE = 128
QK_ROPE = 64
V_DIM = 128
HEAD_SLAB = 256
N_EXPERTS = 16
EC_CAPACITY = 2
GRID_W = 64
SC_WIDTH = 3
CF_WIDTH = 31
CF_PAD = (CF_WIDTH - 1) // 2
HALO = 16
CONV_ROWS = 64
HG_PITCH = 24
ROPE_BASE = 10000.0
NORM_EPS = 1e-6
ATTN_SCALE = (QK_NOPE + QK_ROPE) ** -0.5
LOG2_E = 1.4426950408889634
Q_SCALE = ATTN_SCALE * LOG2_E

F32 = jnp.float32
BF16 = jnp.bfloat16


def _params(sem):
    return pltpu.CompilerParams(dimension_semantics=sem, vmem_limit_bytes=VMEM_LIMIT_BYTES)


def _sigmoid(x):
    return 1.0 / (1.0 + jnp.exp(-x))


def _rms(x, g):
    return x * lax.rsqrt(jnp.mean(x * x, axis=-1, keepdims=True) + NORM_EPS) * g


def _split3_dot(a, b, dims):
    a_hi = a.astype(BF16)
    a_lo = (a - a_hi.astype(F32)).astype(BF16)
    b_hi = b.astype(BF16)
    b_lo = (b - b_hi.astype(F32)).astype(BF16)
    dot = functools.partial(lax.dot_general, dimension_numbers=(dims, ((), ())), preferred_element_type=F32)
    return dot(a_hi, b_hi) + dot(a_lo, b_hi) + dot(a_hi, b_lo)


def _ada_kernel(c_ref, w_ref, b_ref, o_ref):
    cv = c_ref[...]
    s = cv * _sigmoid(cv)
    o_ref[0] = _split3_dot(s, w_ref[0], ((1,), (0,))) + b_ref[0]


def _ada_mod(c_all, ada_w, ada_b):
    depth, d, n6 = ada_w.shape
    rows = c_all.shape[0]
    tn = 1024
    return pl.pallas_call(
        _ada_kernel,
        grid=(depth, n6 // tn),
        in_specs=[pl.BlockSpec((rows, d), lambda l, j: (0, 0)),
                  pl.BlockSpec((1, d, tn), lambda l, j: (l, 0, j)),
                  pl.BlockSpec((1, 1, tn), lambda l, j: (l, 0, j))],
        out_specs=pl.BlockSpec((1, rows, tn), lambda l, j: (l, 0, j)),
        out_shape=jax.ShapeDtypeStruct((depth, rows, n6), F32),
        compiler_params=_params(("parallel", "parallel")),
        name="ada_mod",
    )(c_all, ada_w, ada_b.reshape(depth, 1, n6))


def _moe_residual(x, moe_ref, g2, tm):
    chunks = []
    for p in range(2):
        for c in range(SUBLANES):
            chunks.append(moe_ref[0, p, pl.ds(c, tm, stride=SUBLANES), :])
    return x + g2 * jnp.concatenate(chunks, axis=1)


def _rope_pair(v, tab):
    prod = v * tab
    r = prod + pltpu.roll(prod, QK_ROPE, 1)
    lane = lax.broadcasted_iota(jnp.int32, r.shape, 1)
    return jnp.where(lane < QK_ROPE, r, 0.0)


def _inproj_kernel(has_moe, tm, d_model, q_lora, kv_lora, *refs):
    if has_moe:
        (x_ref, moe_ref, g2_ref, shsc_ref, n1_ref, tab_ref, win_ref, qg_ref, wuq_ref, kvg_ref, wuk_ref, wvt_ref,
         xo_ref, q_ref, k_ref, vt_ref, cv_ref, ckv_ref, kr_ref) = refs
        x = _moe_residual(x_ref[...], moe_ref, g2_ref[0], tm)
        xo_ref[...] = x
    else:
        (x_ref, shsc_ref, n1_ref, tab_ref, win_ref, qg_ref, wuq_ref, kvg_ref, wuk_ref, wvt_ref,
         q_ref, k_ref, vt_ref, cv_ref, ckv_ref, kr_ref) = refs
        x = x_ref[...]
    sh1 = shsc_ref[0, :, :d_model]
    sc1 = shsc_ref[0, :, d_model:]
    h = _rms(x, n1_ref[...]) * (1.0 + sc1) + sh1
    u = jnp.dot(h.astype(BF16), win_ref[...], preferred_element_type=F32)
    tab = tab_ref[...]

    cqn = _rms(u[:, :q_lora], qg_ref[...])
    qf = jnp.dot(cqn.astype(BF16), wuq_ref[...], preferred_element_type=F32)
    for hd in range(N_HEADS):
        o = hd * HEAD_SLAB
        q_ref[:, o:o + QK_NOPE] = (qf[:, o:o + QK_NOPE] * Q_SCALE).astype(BF16)
        q_ref[:, o + QK_NOPE:o + HEAD_SLAB] = (_rope_pair(qf[:, o + QK_NOPE:o + HEAD_SLAB], tab) * Q_SCALE).astype(BF16)

    c0 = q_lora
    ckvn = _rms(u[:, c0:c0 + kv_lora], kvg_ref[...])
    ckv_ref[...] = ckvn
    ckvb = ckvn.astype(BF16)
    knf = jnp.dot(ckvb, wuk_ref[...], preferred_element_type=F32)
    vt_ref[0] = lax.dot_general(wvt_ref[...], ckvb, (((1,), (1,)), ((), ())), preferred_element_type=F32).astype(BF16)
    c1 = c0 + kv_lora
    krv = u[:, c1:c1 + LANES]
    kr_ref[...] = krv[:, :QK_ROPE]
    kr2 = _rope_pair(krv, tab).astype(BF16)
    for hd in range(N_HEADS):
        o = hd * HEAD_SLAB
        k_ref[:, o:o + QK_NOPE] = knf[:, hd * QK_NOPE:(hd + 1) * QK_NOPE].astype(BF16)
        k_ref[:, o + QK_NOPE:o + HEAD_SLAB] = kr2
    cv_ref[...] = u[:, c1 + LANES:]


def _inproj(x, moe, mod_prev, mod, n1, tab, lw, *, tm, tiles_per_mod, tiles_per_seq):
    t, d = x.shape
    nt = t // tm
    p_in = lw["w_in"].shape[1]
    q_lora = lw["q_norm_g"].shape[1]
    kv_lora = lw["kv_norm_g"].shape[1]
    n_cv = p_in - q_lora - kv_lora - LANES
    has_moe = moe is not None
    tiles_per_row = None if moe is None else moe.shape[2] // (tm * SUBLANES)

    full = lambda a: pl.BlockSpec(a.shape, lambda i: (0,) * a.ndim)
    row = lambda w: pl.BlockSpec((tm, w), lambda i: (i, 0))
    in_specs = [row(d)]
    args = [x]
    if has_moe:
        in_specs += [pl.BlockSpec((1, 2, tm * SUBLANES, LANES), lambda i: (i // tiles_per_row, 0, i % tiles_per_row, 0)),
                     pl.BlockSpec((1, 1, d), lambda i: (i // tiles_per_mod, 0, 5))]
        args += [moe, mod_prev]
    in_specs += [pl.BlockSpec((1, 1, 2 * d), lambda i: (i // tiles_per_mod, 0, 0)),
                 full(n1),
                 pl.BlockSpec((tm, LANES), lambda i: (i % tiles_per_seq, 0)),
                 full(lw["w_in"]), full(lw["q_norm_g"]), full(lw["w_uq"]), full(lw["kv_norm_g"]), full(lw["w_uk"]), full(lw["w_vt"])]
    args += [mod, n1, tab, lw["w_in"], lw["q_norm_g"], lw["w_uq"], lw["kv_norm_g"], lw["w_uk"], lw["w_vt"]]

    out_specs, out_shape = [], []
    if has_moe:
        out_specs.append(row(d))
        out_shape.append(jax.ShapeDtypeStruct((t, d), F32))
    seq = tiles_per_seq * tm
    for w, dt in ((N_HEADS * HEAD_SLAB, BF16), (N_HEADS * HEAD_SLAB, BF16), (None, BF16),
                  (n_cv, F32), (kv_lora, F32), (QK_ROPE, F32)):
        if w is None:
            out_specs.append(pl.BlockSpec((1, N_HEADS * V_DIM, tm), lambda i: (i // tiles_per_seq, 0, i % tiles_per_seq)))
            out_shape.append(jax.ShapeDtypeStruct((t // seq, N_HEADS * V_DIM, seq), dt))
        else:
            out_specs.append(row(w))
            out_shape.append(jax.ShapeDtypeStruct((t, w), dt))

    outs = pl.pallas_call(
        functools.partial(_inproj_kernel, has_moe, tm, d, q_lora, kv_lora),
        grid=(nt,), in_specs=in_specs, out_specs=out_specs, out_shape=out_shape,
        compiler_params=_params(("parallel",)), name="inproj",
    )(*args)
    if not has_moe:
        outs = [x] + list(outs)
    return outs


def _cachekv_kernel(ckv_ref, kr_ref, wuk_ref, wvt_ref, k_ref, vt_ref):
    ckvb = ckv_ref[0, 0].astype(BF16)
    knf = jnp.dot(ckvb, wuk_ref[0], preferred_element_type=F32)
    kr2 = kr_ref[0, 0].astype(BF16)
    for hd in range(N_HEADS):
        o = hd * HEAD_SLAB
        k_ref[0, 0, :, o:o + QK_NOPE] = knf[:, hd * QK_NOPE:(hd + 1) * QK_NOPE].astype(BF16)
        k_ref[0, 0, :, o + QK_NOPE:o + HEAD_SLAB] = kr2
    vt_ref[0, 0] = lax.dot_general(wvt_ref[0], ckvb, (((1,), (1,)), ((), ())), preferred_element_type=F32).astype(BF16)


def _cache_kv(ckv_c, kr_c, w_uk_all, w_vt_all):
    depth, b, past, kv_lora = ckv_c.shape
    nk = N_HEADS * HEAD_SLAB
    nv = N_HEADS * V_DIM
    return pl.pallas_call(
        _cachekv_kernel,
        grid=(depth, b),
        in_specs=[pl.BlockSpec((1, 1, past, kv_lora), lambda l, i: (l, i, 0, 0)),
                  pl.BlockSpec((1, 1, past, LANES), lambda l, i: (l, i, 0, 0)),
                  pl.BlockSpec((1, kv_lora, N_HEADS * QK_NOPE), lambda l, i: (l, 0, 0)),
                  pl.BlockSpec((1, nv, kv_lora), lambda l, i: (l, 0, 0))],
        out_specs=[pl.BlockSpec((1, 1, past, nk), lambda l, i: (l, i, 0, 0)),
                   pl.BlockSpec((1, 1, nv, past), lambda l, i: (l, i, 0, 0))],
        out_shape=[jax.ShapeDtypeStruct((depth, b, past, nk), BF16), jax.ShapeDtypeStruct((depth, b, nv, past), BF16)],
        compiler_params=_params(("parallel", "parallel")), name="cache_kv",
    )(ckv_c, kr_c, w_uk_all, w_vt_all)


def _attn_kernel(has_ctx, n_hd, ahead, lk, tk, *refs):
    if has_ctx:
        q_ref, k_ref, vt_ref, kc_ref, vct_ref, o_ref, s_scr, sc_scr = refs
    else:
        q_ref, k_ref, vt_ref, o_ref, s_scr = refs
    tq = q_ref.shape[0]
    n = lk // tk
    qs = [q_ref[:, g * HEAD_SLAB:(g + 1) * HEAD_SLAB] for g in range(n_hd)]

    def scores(g, kc, s_ref):
        s = lax.dot_general(kc, qs[g], (((1,), (1,)), ((), ())), preferred_element_type=F32)
        s_ref[...] = s
        return jnp.max(s, axis=0, keepdims=True)

    def k_chunk(g, c):
        return k_ref[c * tk:(c + 1) * tk, g * HEAD_SLAB:(g + 1) * HEAD_SLAB]

    def vt_chunk(g, c):
        return vt_ref[0, g * V_DIM:(g + 1) * V_DIM, c * tk:(c + 1) * tk]

    def accumulate(state, s_ref, smax, vt):
        m, l, acc = state
        m_new = jnp.maximum(m, smax)
        alpha = jnp.exp2(m - m_new)
        p = jnp.exp2(s_ref[...] - m_new)
        l = alpha * l + jnp.sum(p, axis=0, keepdims=True)
        acc = alpha * acc + jnp.dot(vt, p.astype(BF16), preferred_element_type=F32)
        return m_new, l, acc

    total = n + (1 if has_ctx else 0)
    n_slots = s_scr.shape[1]

    def s_of(g, c):
        return s_scr.at[g, c % n_slots] if c < n else sc_scr.at[g]

    def issue(g, c):
        kc = k_chunk(g, c) if c < n else kc_ref[0, :, g * HEAD_SLAB:(g + 1) * HEAD_SLAB]
        return scores(g, kc, s_of(g, c))

    def vt_of(g, c):
        return vt_chunk(g, c) if c < n else vct_ref[0, g * V_DIM:(g + 1) * V_DIM, :]

    states = [(jnp.full((1, tq), -jnp.inf, F32), jnp.zeros((1, tq), F32), jnp.zeros((V_DIM, tq), F32))] * n_hd
    smax = {}
    for c in range(min(ahead, total)):
        for g in range(n_hd):
            smax[g, c] = issue(g, c)
    for c in range(total):
        for g in range(n_hd):
            if c + ahead < total:
                smax[g, c + ahead] = issue(g, c + ahead)
            states[g] = accumulate(states[g], s_of(g, c), smax.pop((g, c)), vt_of(g, c))
    for g in range(n_hd):
        _, l, acc = states[g]
        o_ref[:, g * V_DIM:(g + 1) * V_DIM] = jnp.transpose(acc / l).astype(BF16)


def _attention(q, k, vt, kc, vct, *, batch, seq, tq, tk, n_hd, ahead):
    t = q.shape[0]
    nq = seq // tq
    has_ctx = kc is not None
    in_specs = [pl.BlockSpec((tq, n_hd * HEAD_SLAB), lambda b, h, i: (b * nq + i, h)),
                pl.BlockSpec((seq, n_hd * HEAD_SLAB), lambda b, h, i: (b, h)),
                pl.BlockSpec((1, n_hd * V_DIM, seq), lambda b, h, i: (b, h, 0))]
    args = [q, k, vt]
    scratch = [pltpu.VMEM((n_hd, ahead + 1, tk, tq), F32)]
    if has_ctx:
        past = kc.shape[1]
        in_specs += [pl.BlockSpec((1, past, n_hd * HEAD_SLAB), lambda b, h, i: (b, 0, h)),
                     pl.BlockSpec((1, n_hd * V_DIM, past), lambda b, h, i: (b, h, 0))]
        args += [kc, vct]
        scratch += [pltpu.VMEM((n_hd, past, tq), F32)]
    return pl.pallas_call(
        functools.partial(_attn_kernel, has_ctx, n_hd, ahead, seq, tk),
        grid=(batch, N_HEADS // n_hd, nq), in_specs=in_specs,
        out_specs=pl.BlockSpec((tq, n_hd * V_DIM), lambda b, h, i: (b * nq + i, h)),
        out_shape=jax.ShapeDtypeStruct((t, N_HEADS * V_DIM), BF16),
        scratch_shapes=scratch,
        compiler_params=_params(("parallel", "parallel", "arbitrary")), name="attention",
    )(*args)


def _mixout_kernel(tm, tiles_per_seq, d_sc, d_cf,
                   o_ref, cv_ref, cvp_ref, cvn_ref, x_ref, g1_ref, sh2_ref, sc2_ref,
                   scw_ref, cfw_ref, cfb_ref, lng_ref, lnb_ref, wo_ref, n2_ref, rw_ref,
                   xo_ref, hg_ref, lg_ref, ext_sc, ext_cf, ph_scr, z_scr, hg_scr):
    i = pl.program_id(0)
    first = (i % tiles_per_seq) == 0
    last = (i % tiles_per_seq) == tiles_per_seq - 1
    o_scx, o_scb, o_scc, o_cfa, o_cfg = 0, d_sc, 2 * d_sc, 3 * d_sc, 3 * d_sc + d_cf

    @pl.when(i == 0)
    def _():
        hg_scr[...] = jnp.zeros_like(hg_scr)

    def sc_in(ref):
        return ref[:, o_scc:o_scc + d_sc] * ref[:, o_scx:o_scx + d_sc]

    def cf_in(ref):
        return ref[:, o_cfa:o_cfa + d_cf] * _sigmoid(ref[:, o_cfg:o_cfg + d_cf])

    ext_sc[0:HALO, :] = jnp.where(first, 0.0, sc_in(cvp_ref))
    ext_sc[HALO:HALO + tm, :] = sc_in(cv_ref)
    ext_sc[HALO + tm:, :] = jnp.where(last, 0.0, sc_in(cvn_ref))
    ext_cf[0:HALO, :] = jnp.where(first, 0.0, cf_in(cvp_ref))
    ext_cf[HALO:HALO + tm, :] = cf_in(cv_ref)
    ext_cf[HALO + tm:, :] = jnp.where(last, 0.0, cf_in(cvn_ref))

    sc_pad = (SC_WIDTH - 1) // 2
    y_sc = jnp.zeros((tm, d_sc), F32)
    for kk in range(SC_WIDTH):
        y_sc = y_sc + scw_ref[kk:kk + 1, :] * ext_sc[pl.ds(HALO - sc_pad + kk, tm), :]
    y_sc = cv_ref[:, o_scb:o_scb + d_sc] * y_sc

    first_row = HALO - CF_PAD
    span = tm + 2 * HALO - SUBLANES
    for phase in range(1, SUBLANES):
        ph_scr[phase - 1, 0:span, :] = ext_cf[pl.ds(phase, span), :]
    for ct in range(d_cf // LANES):
        cs = slice(ct * LANES, (ct + 1) * LANES)
        for rb in range(tm // CONV_ROWS):
            acc = jnp.zeros((CONV_ROWS, LANES), F32)
            for kk in range(CF_WIDTH):
                phase = (first_row + kk) % SUBLANES
                base = first_row + kk - phase + rb * CONV_ROWS
                src = ext_cf if phase == 0 else ph_scr.at[phase - 1]
                acc = acc + cfw_ref[kk:kk + 1, cs] * src[base:base + CONV_ROWS, cs]
            z_scr[rb * CONV_ROWS:(rb + 1) * CONV_ROWS, cs] = acc
    z = z_scr[...] + cfb_ref[...]
    mu = jnp.mean(z, axis=-1, keepdims=True)
    zc = z - mu
    var = jnp.mean(zc * zc, axis=-1, keepdims=True)
    z = zc * lax.rsqrt(var + NORM_EPS) * lng_ref[...] + lnb_ref[...]
    z = z * _sigmoid(z)

    mixed = jnp.concatenate([o_ref[...], y_sc.astype(BF16), z.astype(BF16)], axis=1)
    y = jnp.dot(mixed, wo_ref[...], preferred_element_type=F32)
    x = x_ref[...] + g1_ref[0] * y
    xo_ref[...] = x
    h2 = _rms(x, n2_ref[...]) * (1.0 + sc2_ref[0]) + sh2_ref[0]

    lg = _split3_dot(h2, rw_ref[...], ((1,), (0,)))
    lg_ref[0] = jnp.transpose(lg)[:N_EXPERTS]

    d_model = h2.shape[1]
    for c in range(d_model // LANES):
        hg_scr[pl.ds(c, tm, stride=HG_PITCH), :] = h2[:, c * LANES:(c + 1) * LANES]
    tiles = hg_scr[...].reshape(tm, HG_PITCH, LANES)[:, :BF16_SUBLANES, :]
    hg_ref[...] = tiles.reshape(tm * BF16_SUBLANES, LANES).astype(BF16)


def _mixout(o_att, cv, x, mod, lw, *, tm, tiles_per_mod, tiles_per_seq, route_batch, route_len):
    t, d = x.shape
    nt = t // tm
    n_cv = cv.shape[1]
    d_sc = lw["sc_conv_w"].shape[1]
    d_cf = lw["cf_conv_w"].shape[1]
    hb = tm // HALO
    n_halo = t // HALO
    tiles_per_route = route_len // tm
    full = lambda a: pl.BlockSpec(a.shape, lambda i: (0,) * a.ndim)
    row = lambda w: pl.BlockSpec((tm, w), lambda i: (i, 0))
    modspec = lambda blk: pl.BlockSpec((1, 1, d), lambda i: (i // tiles_per_mod, 0, blk))
    in_specs = [row(o_att.shape[1]), row(n_cv),
                pl.BlockSpec((HALO, n_cv), lambda i: (jnp.maximum(i * hb - 1, 0), 0)),
                pl.BlockSpec((HALO, n_cv), lambda i: (jnp.minimum((i + 1) * hb, n_halo - 1), 0)),
                row(d), modspec(2), modspec(3), modspec(4),
                full(lw["sc_conv_w"]), full(lw["cf_conv_w"]), full(lw["cf_conv_b"]), full(lw["cf_ln_g"]),
                full(lw["cf_ln_b"]), full(lw["w_o"]), full(lw["norm2_g"]), full(lw["router_w"])]
    args = [o_att, cv, cv, cv, x, mod, mod, mod, lw["sc_conv_w"], lw["cf_conv_w"], lw["cf_conv_b"], lw["cf_ln_g"],
            lw["cf_ln_b"], lw["w_o"], lw["norm2_g"], lw["router_w"]]
    out_specs = [row(d),
                 pl.BlockSpec((tm * BF16_SUBLANES, LANES), lambda i: (i, 0)),
                 pl.BlockSpec((1, N_EXPERTS, tm), lambda i: (i // tiles_per_route, 0, i % tiles_per_route))]
    out_shape = [jax.ShapeDtypeStruct((t, d), F32),
                 jax.ShapeDtypeStruct((t * BF16_SUBLANES, LANES), BF16),
                 jax.ShapeDtypeStruct((route_batch, N_EXPERTS, route_len), F32)]
    return pl.pallas_call(
        functools.partial(_mixout_kernel, tm, tiles_per_seq, d_sc, d_cf),
        grid=(nt,), in_specs=in_specs, out_specs=out_specs, out_shape=out_shape,
        scratch_shapes=[pltpu.VMEM((tm + 2 * HALO, d_sc), F32), pltpu.VMEM((tm + 2 * HALO, d_cf), F32),
                        pltpu.VMEM((SUBLANES - 1, tm + 2 * HALO, d_cf), F32), pltpu.VMEM((tm, d_cf), F32),
                        pltpu.VMEM((tm * HG_PITCH, LANES), F32)],
        compiler_params=_params(("arbitrary",)), name="mixout",
    )(*args)


def _route_kernel(seq, cap, jb, lg_ref, idx_ref, gate_ref, slot_scr, aff_scr, hit_scr):
    lg = lg_ref[0]
    ex = jnp.exp(lg - jnp.max(lg, axis=0, keepdims=True))
    aff = ex / jnp.sum(ex, axis=0, keepdims=True)
    bits = pltpu.bitcast(aff, jnp.int32)
    tok = lax.broadcasted_iota(jnp.int32, aff.shape, 1)
    capf = jnp.float32(cap)

    def count(mask):
        return jnp.sum(jnp.where(mask, 1.0, 0.0), axis=1, keepdims=True)

    def greedy_bits(n_bits, keep):
        v = jnp.zeros((N_EXPERTS, 1), jnp.int32)
        bit = n_bits - 1
        while bit >= 0:
            hi = jnp.int32(1 << bit)
            if bit == 0:
                v = jnp.where(keep(v | hi), v | hi, v)
                bit -= 1
            else:
                lo = jnp.int32(1 << (bit - 1))
                k11, k10, k01 = keep(v | hi | lo), keep(v | hi), keep(v | lo)
                v = jnp.where(k11, v | hi | lo, jnp.where(k10, v | hi, jnp.where(k01, v | lo, v)))
                bit -= 2
        return v

    thr = greedy_bits(31, lambda cand: count(bits >= cand) >= capf)
    gt = bits > thr
    eq = bits == thr
    need = capf - count(gt)
    cut = greedy_bits(max(seq.bit_length(), 1), lambda cand: count(eq & (tok < cand)) <= need)
    sel = gt | (eq & (tok < cut))

    r_i = lax.broadcasted_iota(jnp.int32, (LANES, LANES), 0)
    c_i = lax.broadcasted_iota(jnp.int32, (LANES, LANES), 1)
    tri = jnp.where(r_i < c_i, 1.0, 0.0).astype(BF16)
    self = jnp.where(sel, 1.0, 0.0)
    carry = jnp.zeros((N_EXPERTS, 1), F32)
    slots = []
    for blk in range(seq // LANES):
        seg = self[:, blk * LANES:(blk + 1) * LANES]
        excl = jnp.dot(seg.astype(BF16), tri, preferred_element_type=F32)
        slots.append(jnp.where(seg > 0.0, excl + carry, -1.0))
        carry = carry + jnp.sum(seg, axis=1, keepdims=True)
    slot_scr[...] = jnp.concatenate(slots, axis=1)
    aff_scr[...] = aff

    n_chunks = seq // LANES
    jcol = lax.broadcasted_iota(jnp.int32, (jb, LANES), 0).astype(F32)
    lane = lax.broadcasted_iota(jnp.int32, (jb, LANES), 1).astype(F32)

    n_blk = cap // jb
    lead_want = lax.broadcasted_iota(jnp.int32, (n_blk, LANES), 0).astype(F32) * float(jb)
    lead_lane = lax.broadcasted_iota(jnp.int32, (n_blk, LANES), 1).astype(F32)

    for e in range(N_EXPERTS):
        def lead_chunk(cb, acc, e=e):
            off = pl.multiple_of(cb * LANES, LANES)
            hit = slot_scr[e:e + 1, pl.ds(off, LANES)] == lead_want
            return acc + jnp.where(hit, lead_lane + off.astype(F32), 0.0)

        lead = lax.fori_loop(0, n_chunks, lead_chunk, jnp.zeros((n_blk, LANES), F32), unroll=min(n_chunks, 4))
        lead_tok = jnp.sum(lead, axis=1, keepdims=True).astype(jnp.int32)

        for blk in range(n_blk):
            j0 = blk * jb
            want = jcol + float(j0)
            first = lead_tok[blk, 0] // LANES
            last = lead_tok[blk + 1, 0] // LANES if blk + 1 < n_blk else n_chunks - 1

            def per_chunk(cb, carry, e=e, want=want):
                idx_acc, g_acc = carry
                off = pl.multiple_of(cb * LANES, LANES)
                hit = slot_scr[e:e + 1, pl.ds(off, LANES)] == want
                idx_acc = idx_acc + jnp.where(hit, lane + off.astype(F32), 0.0)
                g_acc = g_acc + jnp.where(hit, aff_scr[e:e + 1, pl.ds(off, LANES)], 0.0)
                return idx_acc, g_acc

            zero = jnp.zeros((jb, LANES), F32)
            idx_acc, g_acc = lax.fori_loop(first, last + 1, per_chunk, (zero, zero))
            hit_scr[0, j0:j0 + jb, :] = idx_acc
            hit_scr[1, j0:j0 + jb, :] = g_acc
        idx_ref[0, e] = jnp.sum(hit_scr[0], axis=1, keepdims=True).astype(jnp.int32)
        gate_ref[0, e] = jnp.sum(hit_scr[1], axis=1, keepdims=True)


def _route(logits_t, cap):
    b, e, seq = logits_t.shape
    jb = min(cap, 64)
    return pl.pallas_call(
        functools.partial(_route_kernel, seq, cap, jb),
        grid=(b,),
        in_specs=[pl.BlockSpec((1, e, seq), lambda i: (i, 0, 0))],
        out_specs=[pl.BlockSpec((1, e, cap, 1), lambda i: (i, 0, 0, 0)),
                   pl.BlockSpec((1, e, cap, 1), lambda i: (i, 0, 0, 0))],
        out_shape=[jax.ShapeDtypeStruct((b, e, cap, 1), jnp.int32), jax.ShapeDtypeStruct((b, e, cap, 1), F32)],
        scratch_shapes=[pltpu.VMEM((e, seq), F32), pltpu.VMEM((e, seq), F32), pltpu.VMEM((2, cap, LANES), F32)],
        compiler_params=_params(("parallel",)), name="route",
    )(logits_t)


def _moe_up_kernel(cap, d_model, pitch, idx_ref, hg_ref, wg_ref, wu_ref, hid_ref, xg_scr, xs_scr):
    b, e = pl.program_id(0), pl.program_id(1)

    def gather(expert, slot, j):
        t = idx_ref[(b * N_EXPERTS + expert) * cap + j]
        src = pl.multiple_of(t * BF16_SUBLANES, BF16_SUBLANES)
        tile = hg_ref[0, pl.ds(src, BF16_SUBLANES), :].astype(F32)
        xg_scr[slot, pl.ds(j, SUBLANES, stride=pitch), :] = tile[:SUBLANES]
        xg_scr[slot, pl.ds(SUBLANES * pitch + j, SUBLANES, stride=pitch), :] = tile[SUBLANES:]

    @pl.when(e == 0)
    def _():
        def first(j, _):
            gather(0, 0, j)
            return 0
        lax.fori_loop(0, cap, first, 0, unroll=8)

    slot = e % 2
    for c in range(d_model // LANES):
        xs_scr[:, c * LANES:(c + 1) * LANES] = xg_scr[slot, c * pitch:c * pitch + cap, :].astype(BF16)
    xs = xs_scr[...]
    g = jnp.dot(xs, wg_ref[0, 0], preferred_element_type=F32)
    u = jnp.dot(xs, wu_ref[0, 0], preferred_element_type=F32)
    hid_ref[0, 0] = (g * _sigmoid(g) * u).astype(BF16)
    nxt = jnp.minimum(e + 1, N_EXPERTS - 1)
    for j in range(cap):
        gather(nxt, 1 - slot, j)


def _moe_up(idx_flat, hg, wg, wu, *, layer, cap):
    b, rows, _ = hg.shape
    _, n_e, d_model, d_ff = wg.shape
    pitch = cap + SUBLANES
    grid_spec = pltpu.PrefetchScalarGridSpec(
        num_scalar_prefetch=1, grid=(b, n_e),
        in_specs=[pl.BlockSpec((1, rows, LANES), lambda i, e, idx: (i, 0, 0), pipeline_mode=pl.Buffered(1)),
                  pl.BlockSpec((1, 1, d_model, d_ff), lambda i, e, idx: (layer, e, 0, 0)),
                  pl.BlockSpec((1, 1, d_model, d_ff), lambda i, e, idx: (layer, e, 0, 0))],
        out_specs=pl.BlockSpec((1, 1, cap, d_ff), lambda i, e, idx: (i, e, 0, 0)),
        scratch_shapes=[pltpu.VMEM((2, d_model // LANES * pitch, LANES), F32), pltpu.VMEM((cap, d_model), BF16)])
    return pl.pallas_call(
        functools.partial(_moe_up_kernel, cap, d_model, pitch),
        grid_spec=grid_spec,
        out_shape=jax.ShapeDtypeStruct((b, n_e, cap, d_ff), BF16),
        compiler_params=_params(("arbitrary", "arbitrary")), name="moe_up",
    )(idx_flat, hg, wg, wu)


def _moe_down_kernel(cap, half, pitch, group, idx_ref, hid_ref, gate_ref, wd_ref, acc_ref, og_scr):
    b, e = pl.program_id(0), pl.program_id(2)

    def project(slot):
        o = jnp.dot(hid_ref[0, 0], wd_ref[0, 0], preferred_element_type=F32) * gate_ref[0]
        for c in range(half // LANES):
            og_scr[slot, c * pitch:c * pitch + cap, :] = o[:, c * LANES:(c + 1) * LANES]

    def scatter(expert, slot, j0):
        base = (b * N_EXPERTS + expert) * cap
        dsts = [pl.multiple_of(idx_ref[base + j0 + k], SUBLANES) for k in range(group)]
        vals = [acc_ref[0, 0, pl.ds(dsts[k], SUBLANES), :] + og_scr[slot, pl.ds(j0 + k, SUBLANES, stride=pitch), :]
                for k in range(group)]
        for k in range(group):
            acc_ref[0, 0, pl.ds(dsts[k], SUBLANES), :] = vals[k]

    slot = e % 2

    @pl.when(e == 0)
    def _():
        acc_ref[...] = jnp.zeros_like(acc_ref)
        project(slot)

    @pl.when(e > 0)
    def _():
        for jg in range(cap // group):
            scatter(e - 1, 1 - slot, jg * group)
        project(slot)

    @pl.when(e == N_EXPERTS - 1)
    def _():
        def flush(jg, _):
            scatter(e, slot, jg * group)
            return 0
        lax.fori_loop(0, cap // group, flush, 0)


def _moe_down(idx_flat, hid, gate, wd, *, layer, seq):
    b, n_e, cap, d_ff = hid.shape
    d_model = wd.shape[3]
    half = d_model // 2
    pitch = cap + SUBLANES
    group = min(cap, 8)
    grid_spec = pltpu.PrefetchScalarGridSpec(
        num_scalar_prefetch=1, grid=(b, 2, n_e),
        in_specs=[pl.BlockSpec((1, 1, cap, d_ff), lambda i, p, e, idx: (i, e, 0, 0)),
                  pl.BlockSpec((1, cap, 1), lambda i, p, e, idx: (i * n_e + e, 0, 0)),
                  pl.BlockSpec((1, 1, d_ff, half), lambda i, p, e, idx: (layer, e, 0, p))],
        out_specs=pl.BlockSpec((1, 1, seq * SUBLANES, LANES), lambda i, p, e, idx: (i, p, 0, 0)),
        scratch_shapes=[pltpu.VMEM((2, half // LANES * pitch, LANES), F32)])
    return pl.pallas_call(
        functools.partial(_moe_down_kernel, cap, half, pitch, group),
        grid_spec=grid_spec,
        out_shape=jax.ShapeDtypeStruct((b, 2, seq * SUBLANES, LANES), F32),
        compiler_params=_params(("arbitrary", "arbitrary", "arbitrary")), name="moe_down",
    )(idx_flat, hid, gate, wd)


def _final_kernel(tm, x_ref, moe_ref, g2_ref, g_ref, y_ref):
    x = _moe_residual(x_ref[...], moe_ref, g2_ref[0], tm)
    y_ref[...] = _rms(x, g_ref[...])


def _final(x, moe, mod_prev, g, *, tm, tiles_per_mod):
    t, d = x.shape
    tiles_per_row = moe.shape[2] // (tm * SUBLANES)
    return pl.pallas_call(
        functools.partial(_final_kernel, tm),
        grid=(t // tm,),
        in_specs=[pl.BlockSpec((tm, d), lambda i: (i, 0)),
                  pl.BlockSpec((1, 2, tm * SUBLANES, LANES), lambda i: (i // tiles_per_row, 0, i % tiles_per_row, 0)),
                  pl.BlockSpec((1, 1, d), lambda i: (i // tiles_per_mod, 0, 5)),
                  pl.BlockSpec(g.shape, lambda i: (0, 0))],
        out_specs=pl.BlockSpec((tm, d), lambda i: (i, 0)),
        out_shape=jax.ShapeDtypeStruct((t, d), F32),
        compiler_params=_params(("parallel",)), name="final_norm",
    )(x, moe, mod_prev, g)


def _rope_swap_perm():
    q = QK_ROPE // 4
    return jnp.array(list(range(q, 2 * q)) + list(range(0, q)) + list(range(3 * q, 4 * q)) + list(range(2 * q, 3 * q)),
                     jnp.int32)


def _rope_table(n_tokens):
    rows = n_tokens // GRID_W
    row = jnp.repeat(jnp.arange(rows, dtype=F32), GRID_W)
    col = jnp.tile(jnp.arange(GRID_W, dtype=F32), rows)
    n_pairs = QK_ROPE // 4
    inv = ROPE_BASE ** (-jnp.arange(n_pairs, dtype=F32) / n_pairs)
    ang_r = row[:, None] * inv
    ang_c = col[:, None] * inv
    cr, sr, cc, sn = jnp.cos(ang_r), jnp.sin(ang_r), jnp.cos(ang_c), jnp.sin(ang_c)
    return jnp.concatenate([cr, cr, cc, cc, -sr, sr, -sn, sn], axis=1)


def _layer_weights(l, w):
    d = w["w_in"].shape[1]
    q_lora = w["q_norm_g"].shape[1]
    kv_lora = w["kv_norm_g"].shape[1]
    perm = _rope_swap_perm()
    w_in = w["w_in"][l]
    c1 = q_lora + kv_lora
    kr_w = w_in[:, c1:c1 + QK_ROPE]
    w_in_p = jnp.concatenate([w_in[:, :c1 + QK_ROPE], kr_w[:, perm], w_in[:, c1 + QK_ROPE:]], axis=1).astype(BF16)
    w_uq = w["w_uq"][l]
    rope_w = w_uq[:, :, QK_NOPE:]
    w_uq_p = jnp.concatenate([w_uq, rope_w[:, :, perm]], axis=2).reshape(q_lora, N_HEADS * HEAD_SLAB).astype(BF16)
    w_ukv = w["w_ukv"][l]
    w_uk_p = w_ukv[:, :, :QK_NOPE].reshape(kv_lora, -1).astype(BF16)
    w_vt_p = w_ukv[:, :, QK_NOPE:].reshape(kv_lora, -1).T.astype(BF16)
    return dict(
        w_in=w_in_p, q_norm_g=w["q_norm_g"][l][None], w_uq=w_uq_p, kv_norm_g=w["kv_norm_g"][l][None], w_uk=w_uk_p, w_vt=w_vt_p,
        norm1_g=w["norm1_g"][l][None], norm2_g=w["norm2_g"][l][None],
        sc_conv_w=w["sc_conv_w"][l], cf_conv_w=w["cf_conv_w"][l], cf_conv_b=w["cf_conv_b"][l][None],
        cf_ln_g=w["cf_ln_g"][l][None], cf_ln_b=w["cf_ln_b"][l][None],
        w_o=w["w_o"][l].astype(BF16), router_w=jnp.pad(w["router_w"][l], ((0, 0), (0, LANES - N_EXPERTS))),
        layer=l)


def _run_stream(x0, mods, lws, ew, final_g, *, batch, seq, tab, cache_k, cache_v, merge_moe_rows, tm, tq, tk, n_hd, ahead):
    t, d = x0.shape
    n_mod = mods[0].shape[0]
    tiles_per_mod = t // n_mod // tm
    tiles_per_seq = seq // tm
    cap = max(1, EC_CAPACITY * seq // N_EXPERTS)
    moe_b, moe_seq = (1, t) if merge_moe_rows else (batch, seq)
    moe_cap = cap * (batch // moe_b)
    x, moe = x0, None
    ckvs, krs = [], []
    for l, lw in enumerate(lws):
        x, q, k, v, cv, ckvn, kr = _inproj(x, moe, mods[l - 1] if l else None, mods[l], lw["norm1_g"], tab, lw,
                                           tm=tm, tiles_per_mod=tiles_per_mod, tiles_per_seq=tiles_per_seq)
        ckvs.append(ckvn)
        krs.append(kr)
        kc = None if cache_k is None else cache_k[l]
        vc = None if cache_v is None else cache_v[l]
        o_att = _attention(q, k, v, kc, vc, batch=batch, seq=seq, tq=tq, tk=tk, n_hd=n_hd, ahead=ahead)
        x, hg, logits_t = _mixout(o_att, cv, x, mods[l], lw, tm=tm, tiles_per_mod=tiles_per_mod,
                                  tiles_per_seq=tiles_per_seq, route_batch=batch, route_len=seq)
        idx, gate = _route(logits_t, cap)
        if moe_b != batch:
            offs = (jnp.arange(batch, dtype=jnp.int32) * seq)[:, None, None, None]
            idx = jnp.transpose(idx + offs, (1, 0, 2, 3)).reshape(1, N_EXPERTS, moe_cap, 1)
            gate = jnp.transpose(gate, (1, 0, 2, 3)).reshape(1, N_EXPERTS, moe_cap, 1)
        idx_flat = idx.reshape(-1)
        gate = gate.reshape(moe_b * N_EXPERTS, moe_cap, 1)
        hid = _moe_up(idx_flat, hg.reshape(moe_b, moe_seq * BF16_SUBLANES, LANES), ew["wg"], ew["wu"], layer=lw["layer"], cap=moe_cap)
        moe = _moe_down(idx_flat * SUBLANES, hid, gate, ew["wd"], layer=lw["layer"], seq=moe_seq)
    y = _final(x, moe, mods[-1], final_g, tm=tm, tiles_per_mod=tiles_per_mod)
    return y, ckvs, krs


def kernel(x_prompt, x_sample, cache_ckv, cache_krope, c, c_ctx, ada_w, ada_b, norm1_g, norm2_g, w_in, q_norm_g, w_uq,
           kv_norm_g, w_ukv, sc_conv_w, cf_conv_w, cf_conv_b, cf_ln_g, cf_ln_b, w_o, router_w, exp_w_gate, exp_w_up,
           exp_w_down, final_norm_g):
    batch, seq, d = x_prompt.shape
    dec_batch, dec_seq, _ = x_sample.shape
    depth = ada_w.shape[0]
    past = cache_ckv.shape[2]
    w = dict(norm1_g=norm1_g, norm2_g=norm2_g, w_in=w_in, q_norm_g=q_norm_g, w_uq=w_uq, kv_norm_g=kv_norm_g, w_ukv=w_ukv,
             sc_conv_w=sc_conv_w, cf_conv_w=cf_conv_w, cf_conv_b=cf_conv_b, cf_ln_g=cf_ln_g, cf_ln_b=cf_ln_b, w_o=w_o,
             router_w=router_w, exp_w_gate=exp_w_gate, exp_w_up=exp_w_up, exp_w_down=exp_w_down)
    lws = [_layer_weights(l, w) for l in range(depth)]
    ew = dict(wg=exp_w_gate.astype(BF16), wu=exp_w_up.astype(BF16), wd=exp_w_down.astype(BF16))

    n_rows = -(-(1 + dec_batch) // SUBLANES) * SUBLANES
    c_all = jnp.zeros((n_rows, d), F32).at[0].set(c_ctx).at[1:1 + dec_batch].set(c)
    mod = _ada_mod(c_all, ada_w, ada_b)
    mods_p = [mod[l, 0:1][:, None, :] for l in range(depth)]
    mods_s = [mod[l, 1:1 + dec_batch][:, None, :] for l in range(depth)]

    tm = 256
    final_g = final_norm_g[None]

    ident = jnp.concatenate([jnp.ones((tm, QK_ROPE), F32), jnp.zeros((tm, QK_ROPE), F32)], axis=1)
    y_p, ckvs, krs = _run_stream(x_prompt.reshape(batch * seq, d), mods_p, lws, ew, final_g, batch=batch, seq=seq, tab=ident,
                                 cache_k=None, cache_v=None, merge_moe_rows=True, tm=tm, tq=min(seq, 256), tk=min(seq, 512), n_hd=N_HEADS, ahead=1)

    ckv_c = jnp.transpose(cache_ckv, (1, 0, 2, 3))
    kr_c = jnp.transpose(cache_krope, (1, 0, 2, 3))
    kr_c = jnp.pad(kr_c, ((0, 0), (0, 0), (0, 0), (0, LANES - kr_c.shape[-1])))
    ck, cvv = _cache_kv(ckv_c, kr_c, jnp.stack([lw["w_uk"] for lw in lws]), jnp.stack([lw["w_vt"] for lw in lws]))
    y_s, _, _ = _run_stream(x_sample.reshape(dec_batch * dec_seq, d), mods_s, lws, ew, final_g, batch=dec_batch, seq=dec_seq,
                            tab=_rope_table(dec_seq), cache_k=ck, cache_v=cvv, merge_moe_rows=False, tm=tm, tq=256, tk=min(512, dec_seq // 4), n_hd=4, ahead=3)

    new_ckv = jnp.stack([a.reshape(batch, seq, -1) for a in ckvs], axis=1)
    new_kr = jnp.stack([a.reshape(batch, seq, -1) for a in krs], axis=1)
    return (y_p.reshape(batch, seq, d), y_s.reshape(dec_batch, dec_seq, d), new_ckv, new_kr)
```
